```python
import jax, jax.numpy as jnp
from jax import lax
import numpy as np

D_MODEL = 1024
BATCH = 32
SEQ = 2048
DEPTH = 2

CHUNK = 64
HG_HEADS = 4
HG_DK = 128
HG_DV = 128
HG_WIDTH = HG_HEADS * HG_DK
ML_HEADS = 4
ML_DV = 128
ML_DQK = 64
ML_WIDTH = ML_HEADS * ML_DV
CONV_K = 4
N_BRANCH = 2
IN_SIZES = (HG_WIDTH, HG_WIDTH, HG_WIDTH, HG_WIDTH, ML_WIDTH, ML_WIDTH, ML_WIDTH, ML_HEADS, ML_HEADS, D_MODEL, D_MODEL)
IN_COLS = 4 * HG_WIDTH + 3 * ML_WIDTH + 2 * ML_HEADS + N_BRANCH * D_MODEL
D_FF = 2816
N_EXPERTS = 8
TOP_K = 2
D_FF_EXPERT = 1408
N_DENSE = (DEPTH + 1) // 2
N_MOE = DEPTH // 2
EPS = 1e-6

kernel_name = "hgrn2_mlstm_gated_hybrid_moe"


def rms_norm(x, g):
    xf = x.astype(jnp.float32)
    y = xf * lax.rsqrt(jnp.mean(xf * xf, axis=-1, keepdims=True) + EPS)
    return (y * g.astype(jnp.float32)).astype(x.dtype)


def head_rms_norm(o, g):
    B, S, H, d = o.shape
    y = o * lax.rsqrt(jnp.mean(o * o, axis=-1, keepdims=True) + EPS)
    return y.reshape(B, S, H * d) * g.astype(jnp.float32)


def ada_params(c_act, w, b):
    mod = c_act @ w + b
    shift, scale, gate = jnp.split(mod, 3, axis=-1)
    return shift[:, None, :], scale[:, None, :], gate[:, None, :]


def modulate(h, shift, scale):
    return h * (1 + scale) + shift


def to_chunks(t):
    B, S, H, d = t.shape
    return t.reshape(B, S // CHUNK, CHUNK, H, d).transpose(1, 0, 3, 2, 4)


def from_chunks(t):
    nc, B, H, C, d = t.shape
    return t.transpose(1, 0, 3, 2, 4).reshape(B, nc * C, H, d)


def hgrn2_chunkwise(q, k, v, log_f):
    B, S, H, DK = q.shape
    DV = v.shape[-1]
    qc, kc, vc, lfc = (to_chunks(t.astype(jnp.float32)) for t in (q, k, v, log_f))
    causal = jnp.tril(jnp.ones((CHUNK, CHUNK), dtype=bool))

    def step(state, inp):
        qb, kb, vb, lfb = inp
        A = jnp.cumsum(lfb, axis=2)
        diff = A[:, :, :, None, :] - A[:, :, None, :, :]
        decay = jnp.exp(jnp.where(causal[:, :, None], diff, -jnp.inf))
        scores = jnp.einsum("bhtk,bhsk,bhtsk->bhts", qb, kb, decay)
        o_intra = jnp.einsum("bhts,bhsv->bhtv", scores, vb)
        o_inter = jnp.einsum("bhtk,bhkv->bhtv", qb * jnp.exp(A), state)
        A_last = A[:, :, -1:, :]
        k_dec = kb * jnp.exp(A_last - A)
        new_state = jnp.exp(A_last[:, :, 0, :])[..., None] * state + jnp.einsum("bhsk,bhsv->bhkv", k_dec, vb)
        return new_state, o_intra + o_inter

    s0 = jnp.zeros((B, H, DK, DV), jnp.float32)
    _, o = lax.scan(step, s0, (qc, kc, vc, lfc))
    return from_chunks(o)


def mlstm_chunkwise(q, k, v, i_pre, log_f):
    B, S, H, DQK = q.shape
    DV = v.shape[-1]
    qc, kc, vc = (to_chunks(t.astype(jnp.float32)) for t in (q, k, v))
    ic, fc = (to_chunks(t.astype(jnp.float32)[..., None])[..., 0] for t in (i_pre, log_f))
    causal = jnp.tril(jnp.ones((CHUNK, CHUNK), dtype=bool))

    def step(carry, inp):
        C_prev, n_prev, m_prev = carry
        qb, kb, vb, ib, lfb = inp
        b = jnp.cumsum(lfb, axis=-1)
        d_log = jnp.where(causal, b[..., :, None] - b[..., None, :] + ib[..., None, :], -jnp.inf)
        inter_log = b + m_prev[..., None]
        m_t = jnp.maximum(jnp.max(d_log, axis=-1), inter_log)
        w = jnp.exp(d_log - m_t[..., None])
        w_inter = jnp.exp(inter_log - m_t)
        qk = jnp.einsum("bhtd,bhsd->bhts", qb, kb) * w
        num = jnp.einsum("bhts,bhsv->bhtv", qk, vb) + w_inter[..., None] * jnp.einsum("bhtd,bhdv->bhtv", qb, C_prev)
        den = jnp.sum(qk, axis=-1) + w_inter * jnp.einsum("bhtd,bhd->bht", qb, n_prev)
        h = num / jnp.maximum(jnp.abs(den), jnp.exp(-m_t))[..., None]
        b_last = b[..., -1]
        state_log = b_last[..., None] - b + ib
        m_new = jnp.maximum(b_last + m_prev, jnp.max(state_log, axis=-1))
        w_s = jnp.exp(state_log - m_new[..., None])
        decay_prev = jnp.exp(b_last + m_prev - m_new)
        C_new = decay_prev[..., None, None] * C_prev + jnp.einsum("bhs,bhsd,bhsv->bhdv", w_s, kb, vb)
        n_new = decay_prev[..., None] * n_prev + jnp.einsum("bhs,bhsd->bhd", w_s, kb)
        return (C_new, n_new, m_new), h

    carry0 = (jnp.zeros((B, H, DQK, DV), jnp.float32), jnp.zeros((B, H, DQK), jnp.float32), jnp.zeros((B, H), jnp.float32))
    _, h = lax.scan(step, carry0, (qc, kc, vc, ic, fc))
    return from_chunks(h)


def causal_dwconv(u, w, b):
    S = u.shape[1]
    up = jnp.pad(u, ((0, 0), (CONV_K - 1, 0), (0, 0)))
    y = b + w[0] * up[:, 0:S]
    for j in range(1, CONV_K):
        y = y + w[j] * up[:, j:j + S]
    return y


def hybrid_mixer(h, w_in, lb, hg_norm_g, conv_w, conv_b, wq, wk, fbias, ml_norm_g, w_br_hg, w_br_ml, w_out):
    B, S, _ = h.shape
    proj = h @ w_in
    offs = []
    acc = 0
    for sz in IN_SIZES[:-1]:
        acc += sz
        offs.append(acc)
    hq, hf, hi, hg, mu, mv, mo, mi, mf, ga, gb = jnp.split(proj, offs, axis=-1)

    zf = hf.astype(jnp.float32)
    log_f = jnp.logaddexp(jnp.log(lb), jnp.log1p(-lb) + jax.nn.log_sigmoid(zf))
    k_hg = (1 - lb) * jax.nn.sigmoid(-zf)
    q_hg = jax.nn.silu(hq.astype(jnp.float32))
    hs = lambda t, d: t.reshape(B, S, HG_HEADS, d)
    o_hg = hgrn2_chunkwise(hs(q_hg, HG_DK), hs(k_hg, HG_DK), hs(hi, HG_DV), hs(log_f, HG_DK))
    y_hg = head_rms_norm(o_hg, hg_norm_g) * jax.nn.silu(hg.astype(jnp.float32))

    uc = jax.nn.silu(causal_dwconv(mu, conv_w, conv_b)).reshape(B, S, ML_HEADS, ML_DV)
    q_ml = jnp.einsum("bshd,hde->bshe", uc, wq) * (ML_DQK ** -0.5)
    k_ml = jnp.einsum("bshd,hde->bshe", uc, wk)
    v_ml = mv.reshape(B, S, ML_HEADS, ML_DV)
    log_f_ml = jax.nn.log_sigmoid(mf.astype(jnp.float32) + fbias)
    o_ml = mlstm_chunkwise(q_ml, k_ml, v_ml, mi, log_f_ml)
    y_ml = jax.nn.sigmoid(mo.astype(jnp.float32)) * head_rms_norm(o_ml, ml_norm_g)

    y_hg = y_hg.astype(h.dtype) @ w_br_hg
    y_ml = y_ml.astype(h.dtype) @ w_br_ml
    merged = jax.nn.sigmoid(ga) * y_hg + jax.nn.sigmoid(gb) * y_ml
    return merged @ w_out


def swiglu(h, w1, w3, w2):
    return (jax.nn.silu(h @ w1) * (h @ w3)) @ w2


def moe_swiglu(h, router_w, router_b, w1, w3, w2):
    B, S, D = h.shape
    t = h.reshape(B * S, D)
    logits = (t @ router_w).astype(jnp.float32) + router_b.astype(jnp.float32)
    top_v, top_i = lax.top_k(logits, TOP_K)
    gates = jax.nn.softmax(top_v, axis=-1)
    combine = jnp.sum(jax.nn.one_hot(top_i, N_EXPERTS, dtype=jnp.float32) * gates[..., None], axis=1)
    combine = combine.astype(h.dtype)
    out = combine[:, 0:1] * swiglu(t, w1[0], w3[0], w2[0])
    for e in range(1, N_EXPERTS):
        out = out + combine[:, e:e + 1] * swiglu(t, w1[e], w3[e], w2[e])
    return out.reshape(B, S, D)


def setup_inputs(seed: int = 0) -> dict:
    key = jax.random.key(seed)
    ks = jax.random.split(key, 32)
    D = D_MODEL

    def nrm(k, shape, s):
        return jax.random.normal(k, shape, jnp.float32) * s

    return {
        "x": nrm(ks[0], (BATCH, SEQ, D), 1.0),
        "c": nrm(ks[1], (BATCH, D), 1.0),
        "norm_mix_g": 1.0 + nrm(ks[2], (DEPTH, D), 0.1),
        "ada_mix_w": nrm(ks[3], (DEPTH, D, 3 * D), 0.5 * D ** -0.5),
        "ada_mix_b": nrm(ks[4], (DEPTH, 3 * D), 0.02),
        "w_in": nrm(ks[5], (DEPTH, D, IN_COLS), D ** -0.5),
        "hg_lb_logits": nrm(ks[6], (DEPTH, HG_WIDTH), 1.0),
        "hg_norm_g": 1.0 + nrm(ks[7], (DEPTH, HG_WIDTH), 0.1),
        "ml_conv_w": nrm(ks[8], (DEPTH, CONV_K, ML_WIDTH), CONV_K ** -0.5),
        "ml_conv_b": nrm(ks[9], (DEPTH, ML_WIDTH), 0.02),
        "ml_wq": nrm(ks[10], (DEPTH, ML_HEADS, ML_DV, ML_DQK), ML_DV ** -0.5),
        "ml_wk": nrm(ks[11], (DEPTH, ML_HEADS, ML_DV, ML_DQK), ML_DV ** -0.5),
        "ml_fbias": 3.0 + 3.0 * jax.random.uniform(ks[12], (DEPTH, ML_HEADS), jnp.float32),
        "ml_norm_g": 1.0 + nrm(ks[13], (DEPTH, ML_WIDTH), 0.1),
        "w_br_hg": nrm(ks[14], (DEPTH, HG_WIDTH, D), HG_WIDTH ** -0.5),
        "w_br_ml": nrm(ks[15], (DEPTH, ML_WIDTH, D), ML_WIDTH ** -0.5),
        "w_out": nrm(ks[16], (DEPTH, D, D), D ** -0.5),
        "norm_ffn_g": 1.0 + nrm(ks[17], (DEPTH, D), 0.1),
        "ada_ffn_w": nrm(ks[18], (DEPTH, D, 3 * D), 0.5 * D ** -0.5),
        "ada_ffn_b": nrm(ks[19], (DEPTH, 3 * D), 0.02),
        "ffn_w1": nrm(ks[20], (N_DENSE, D, D_FF), D ** -0.5),
        "ffn_w3": nrm(ks[21], (N_DENSE, D, D_FF), D ** -0.5),
        "ffn_w2": nrm(ks[22], (N_DENSE, D_FF, D), D_FF ** -0.5),
        "moe_router_w": nrm(ks[23], (N_MOE, D, N_EXPERTS), D ** -0.5),
        "moe_router_b": nrm(ks[24], (N_MOE, N_EXPERTS), 0.01),
        "moe_w1": nrm(ks[25], (N_MOE, N_EXPERTS, D, D_FF_EXPERT), D ** -0.5),
        "moe_w3": nrm(ks[26], (N_MOE, N_EXPERTS, D, D_FF_EXPERT), D ** -0.5),
        "moe_w2": nrm(ks[27], (N_MOE, N_EXPERTS, D_FF_EXPERT, D), D_FF_EXPERT ** -0.5),
        "final_norm_g": 1.0 + nrm(ks[28], (D,), 0.1),
    }


def reference(x, c, norm_mix_g, ada_mix_w, ada_mix_b, w_in, hg_lb_logits, hg_norm_g,
              ml_conv_w, ml_conv_b, ml_wq, ml_wk, ml_fbias, ml_norm_g, w_br_hg, w_br_ml, w_out,
              norm_ffn_g, ada_ffn_w, ada_ffn_b, ffn_w1, ffn_w3, ffn_w2,
              moe_router_w, moe_router_b, moe_w1, moe_w3, moe_w2, final_norm_g):
    c_act = jax.nn.silu(c)
    lb_all = jnp.cumsum(jax.nn.softmax(hg_lb_logits.astype(jnp.float32), axis=0), axis=0)
    lb_all = lb_all - lb_all[0:1]
    for l in range(DEPTH):
        shift, scale, gate = ada_params(c_act, ada_mix_w[l], ada_mix_b[l])
        h = modulate(rms_norm(x, norm_mix_g[l]), shift, scale)
        mix = hybrid_mixer(h, w_in[l], lb_all[l], hg_norm_g[l], ml_conv_w[l], ml_conv_b[l],
                           ml_wq[l], ml_wk[l], ml_fbias[l], ml_norm_g[l], w_br_hg[l], w_br_ml[l], w_out[l])
        x = x + gate * mix
        shift, scale, gate = ada_params(c_act, ada_ffn_w[l], ada_ffn_b[l])
        h = modulate(rms_norm(x, norm_ffn_g[l]), shift, scale)
        j = l // 2
        if l % 2 == 0:
            ff = swiglu(h, ffn_w1[j], ffn_w3[j], ffn_w2[j])
        else:
            ff = moe_swiglu(h, moe_router_w[j], moe_router_b[j], moe_w1[j], moe_w3[j], moe_w2[j])
        x = x + gate * ff
    return rms_norm(x, final_norm_g)
```

```python
import functools

import jax
import jax.numpy as jnp
from jax import lax
from jax.experimental import pallas as pl
from jax.experimental.pallas import tpu as pltpu

F32 = jnp.float32
BF16 = jnp.bfloat16

EPS = 1e-6
LANES = 128
HG_HEADS = 4
HG_D = 128
HG_W = HG_HEADS * HG_D
ML_HEADS = 4
ML_DV = 128
ML_DQK = 64
ML_W = ML_HEADS * ML_DV
CONV_K = 4
N_EXPERTS = 8
HG_CHUNK = 64
HG_SUB = 16
FF_TILE = 1408
VMEM_LIMIT = 56 * 1024 * 1024


def _cparams(sem):
    return pltpu.CompilerParams(dimension_semantics=sem, vmem_limit_bytes=VMEM_LIMIT)


def _const_spec(shape):
    nd = len(shape)
    return pl.BlockSpec(shape, lambda *_: (0,) * nd, pipeline_mode=pl.Buffered(1))


def _silu(v):
    return v * jax.nn.sigmoid(v)


def _log_sigmoid(z):
    return jnp.minimum(z, 0.0) - jnp.log1p(jnp.exp(-jnp.abs(z)))


def _split_bf16(v):
    hi = v.astype(BF16)
    lo = (v - hi.astype(F32)).astype(BF16)
    return hi, lo


def _dot(a, b):
    return jnp.dot(a, b, preferred_element_type=F32)


def _dot_nt(a, b):
    return lax.dot_general(a, b, (((1,), (1,)), ((), ())), preferred_element_type=F32)


def _dot_tn(a, b):
    return lax.dot_general(a, b, (((0,), (0,)), ((), ())), preferred_element_type=F32)


def _ada_kernel(c_ref, wm_ref, bm_ref, wf_ref, bf_ref, om_ref, of_ref):
    c = c_ref[...]
    ca = _silu(c).astype(BF16)
    om_ref[...] = _dot(ca, wm_ref[...].astype(BF16)) + bm_ref[...]
    of_ref[...] = _dot(ca, wf_ref[...].astype(BF16)) + bf_ref[...]


def _ada_params(c, wm, bm, wf, bf):
    depth, d, d3 = wm.shape
    b = c.shape[0]
    nt = d3 // d
    w_spec = pl.BlockSpec((None, d, d), lambda l, j: (l, 0, j))
    b_spec = pl.BlockSpec((None, 1, d), lambda l, j: (l, 0, j))
    o_spec = pl.BlockSpec((None, b, d), lambda l, j: (l, 0, j))
    out = jax.ShapeDtypeStruct((depth, b, d3), F32)
    return pl.pallas_call(
        _ada_kernel,
        grid=(depth, nt),
        in_specs=[pl.BlockSpec((b, d), lambda l, j: (0, 0)), w_spec, b_spec, w_spec, b_spec],
        out_specs=[o_spec, o_spec],
        out_shape=[out, out],
        compiler_params=_cparams(("arbitrary", "arbitrary")),
        name="ada_params",
    )(c, wm, bm.reshape(depth, 1, d3), wf, bf.reshape(depth, 1, d3))


def _norm_modulate(x, g, shift, scale):
    ms = jnp.mean(x * x, axis=-1, keepdims=True)
    y = x * lax.rsqrt(ms + EPS) * g
    return y * (1.0 + scale) + shift


def _proj_kernel(x_ref, mod_ref, g_ref, wm_ref, wg_ref, wif_ref, lbp_ref,
                 q_ref, k_ref, v_ref, gs_ref, lf_ref, mu_ref, mv_ref, mo_ref, ga_ref, gb_ref, mif_ref):
    d = x_ref.shape[-1]
    mod = mod_ref[...]
    h = _norm_modulate(x_ref[...], g_ref[...], mod[:, :d], mod[:, d:2 * d]).astype(BF16)

    def piece(w_ref, j, width=HG_W):
        return _dot(h, w_ref[:, j * width:(j + 1) * width])

    q_ref[...] = _silu(piece(wm_ref, 0)).astype(BF16)
    zf = piece(wm_ref, 1)
    log_lb, log1m_lb, one_m_lb = lbp_ref[0:1, :], lbp_ref[1:2, :], lbp_ref[2:3, :]
    t = log1m_lb + _log_sigmoid(zf)
    lf_ref[...] = jnp.maximum(log_lb, t) + jnp.log1p(jnp.exp(-jnp.abs(log_lb - t)))
    k_ref[...] = (one_m_lb * jax.nn.sigmoid(-zf)).astype(BF16)
    v_ref[...] = piece(wm_ref, 2).astype(BF16)
    gs_ref[...] = _silu(piece(wm_ref, 3)).astype(BF16)
    mu_ref[...] = piece(wm_ref, 4).astype(BF16)
    mv_ref[...] = piece(wm_ref, 5).astype(BF16)
    mo_ref[...] = jax.nn.sigmoid(piece(wm_ref, 6)).astype(BF16)
    ga_ref[...] = jax.nn.sigmoid(piece(wg_ref, 0, d)).astype(BF16)
    gb_ref[...] = jax.nn.sigmoid(piece(wg_ref, 1, d)).astype(BF16)
    mif_ref[...] = _dot(h, wif_ref[...])


def _proj(x2, mod, g, wm, wg, wif, lbp, seq, tm):
    t, d = x2.shape
    per_b = seq // tm
    row = lambda w: pl.BlockSpec((tm, w), lambda i: (i, 0))
    sds = lambda w, dt: jax.ShapeDtypeStruct((t, w), dt)
    outs = [(HG_W, BF16)] * 4 + [(HG_W, F32)] + [(ML_W, BF16)] * 3 + [(d, BF16)] * 2 + [(LANES, F32)]
    return pl.pallas_call(
        _proj_kernel,
        grid=(t // tm,),
        in_specs=[row(d),
                  pl.BlockSpec((None, 1, mod.shape[-1]), lambda i: (i // per_b, 0, 0)),
                  _const_spec(g.shape), _const_spec(wm.shape), _const_spec(wg.shape),
                  _const_spec(wif.shape), _const_spec(lbp.shape)],
        out_specs=[row(w) for w, _ in outs],
        out_shape=[sds(w, dt) for w, dt in outs],
        compiler_params=_cparams(("arbitrary",)),
        name="in_proj",
    )(x2, mod, g, wm, wg, wif, lbp)


def _head_norm(o, g):
    return o * lax.rsqrt(jnp.mean(o * o, axis=-1, keepdims=True) + EPS) * g


def _hgrn2_kernel(q_ref, k_ref, v_ref, lf_ref, gs_ref, g_ref, tri_ref, sel_ref, y_ref, st_ref):
    ts = q_ref.shape[0]
    nsub = HG_CHUNK // HG_SUB

    @pl.when(pl.program_id(1) == 0)
    def _():
        st_ref[...] = jnp.zeros_like(st_ref)

    tri = tri_ref[...]
    row = lax.broadcasted_iota(jnp.int32, (HG_CHUNK, HG_CHUNK), 0)
    col = lax.broadcasted_iota(jnp.int32, (HG_CHUNK, HG_CHUNK), 1)
    diag_mask = (row // HG_SUB == col // HG_SUB) & (col <= row)

    def chunk(c, carry):
        r0 = pl.multiple_of(c * HG_CHUNK, HG_CHUNK)
        rows = pl.ds(r0, HG_CHUNK)
        lf = lf_ref[rows, :]
        hi, lo = _split_bf16(lf)
        a_all = _dot(tri, hi) + _dot(tri, lo)
        p_heads = []
        per_head = []
        for hh in range(HG_HEADS):
            cols = slice(hh * HG_D, (hh + 1) * HG_D)
            a = a_all[:, cols]
            qf = q_ref[rows, cols].astype(F32)
            kf = k_ref[rows, cols].astype(F32)
            a4 = a.reshape(nsub, HG_SUB, HG_D)
            q4 = qf.reshape(nsub, HG_SUB, HG_D)
            k4 = kf.reshape(nsub, HG_SUB, HG_D)
            pieces = []
            for s in range(HG_SUB):
                dec = jnp.exp(jnp.minimum(a4 - a4[:, s:s + 1, :], 0.0))
                pieces.append((q4 * k4[:, s:s + 1, :] * dec).reshape(HG_CHUNK, HG_D).astype(BF16))
            p_heads.append(jnp.concatenate(pieces, axis=1))
            per_head.append((a, qf, kf, cols))
        diag = _dot(jnp.concatenate(p_heads, axis=0), sel_ref[...])
        for hh, (a, qf, kf, cols) in enumerate(per_head):
            vb = v_ref[rows, cols]
            a_last = a[HG_CHUNK - 1:HG_CHUNK, :]
            qs, ks = [], []
            for j in range(nsub - 1):
                e = a[(j + 1) * HG_SUB - 1:(j + 1) * HG_SUB, :]
                lo_r, hi_r = j * HG_SUB, (j + 1) * HG_SUB
                kt = kf[lo_r:hi_r] * jnp.exp(e - a[lo_r:hi_r])
                qt = qf[hi_r:] * jnp.exp(a[hi_r:] - e)
                ks.append(jnp.concatenate(
                    [jnp.zeros((lo_r, HG_D), F32), kt, jnp.zeros((HG_CHUNK - hi_r, HG_D), F32)], axis=0)
                    if lo_r else jnp.concatenate([kt, jnp.zeros((HG_CHUNK - hi_r, HG_D), F32)], axis=0))
                qs.append(jnp.concatenate([jnp.zeros((hi_r, HG_D), F32), qt], axis=0))
            s_off = _dot_nt(jnp.concatenate(qs, axis=1).astype(BF16), jnp.concatenate(ks, axis=1).astype(BF16))
            s_diag = diag[hh * HG_CHUNK:(hh + 1) * HG_CHUNK, :HG_CHUNK]
            s_full = s_off + jnp.where(diag_mask, s_diag, 0.0)
            st = st_ref[hh]
            o = _dot(s_full.astype(BF16), vb) + _dot_nt((qf * jnp.exp(a)).astype(BF16), st.astype(BF16))
            k_dec = (kf * jnp.exp(a_last - a)).astype(BF16)
            st_ref[hh] = st * jnp.exp(a_last) + _dot_tn(vb, k_dec)
            y = _head_norm(o, g_ref[:, cols]) * gs_ref[rows, cols].astype(F32)
            y_ref[rows, cols] = y.astype(BF16)
        return carry

    lax.fori_loop(0, ts // HG_CHUNK, chunk, 0)


def _hgrn2(q, k, v, lf, gs, g, batch, seq, ts):
    t = q.shape[0]
    per_b = seq // ts
    tri = jnp.tril(jnp.ones((HG_CHUNK, HG_CHUNK), F32)).astype(BF16)
    r = jnp.arange(HG_SUB * HG_D)[:, None] // HG_D
    cidx = jnp.arange(LANES)[None, :]
    sel = ((cidx % HG_SUB == r) & (cidx < HG_CHUNK)).astype(BF16)
    row = pl.BlockSpec((ts, HG_W), lambda b, s: (b * per_b + s, 0))
    return pl.pallas_call(
        _hgrn2_kernel,
        grid=(batch, per_b),
        in_specs=[row, row, row, row, row, _const_spec(g.shape), _const_spec(tri.shape), _const_spec(sel.shape)],
        out_specs=row,
        out_shape=jax.ShapeDtypeStruct((t, HG_W), BF16),
        scratch_shapes=[pltpu.VMEM((HG_HEADS, HG_D, HG_D), F32)],
        compiler_params=_cparams(("arbitrary", "arbitrary")),
        name="hgrn2",
    )(q, k, v, lf, gs, g, tri, sel)


def _mlstm_kernel(mu_ref, mv_ref, mo_ref, mif_ref, cw_ref, cb_ref, wq_ref, wk_ref, fb_ref, g_ref, tri_ref,
                  y_ref, carry_ref, c_ref, m_ref):
    ts = mu_ref.shape[0]

    @pl.when(pl.program_id(1) == 0)
    def _():
        carry_ref[...] = jnp.zeros_like(carry_ref)
        c_ref[...] = jnp.zeros_like(c_ref)
        m_ref[...] = jnp.zeros_like(m_ref)

    u = mu_ref[...].astype(F32)
    ext = jnp.concatenate([carry_ref[...], u], axis=0)
    carry_ref[...] = u[ts - 8:, :]
    conv = cb_ref[...]
    for j in range(CONV_K):
        off = 8 - (CONV_K - 1) + j
        conv = conv + cw_ref[j:j + 1, :] * ext[off:off + ts, :]
    uc = _silu(conv).astype(BF16)

    mif = mif_ref[...]
    lf = _log_sigmoid(mif + fb_ref[...])
    hi, lo = _split_bf16(lf)
    tri = tri_ref[...]
    bcum = _dot(tri, hi) + _dot(tri, lo)
    lane = lax.broadcasted_iota(jnp.int32, mif.shape, 1)
    z = jnp.where(lane < ML_HEADS, mif, bcum)
    zt = z.T

    row = lax.broadcasted_iota(jnp.int32, (ts, ts), 0)
    col = lax.broadcasted_iota(jnp.int32, (ts, ts), 1)
    causal = col <= row
    ones_col = (lax.broadcasted_iota(jnp.int32, (ts, ML_DV), 1) == 0).astype(BF16)

    for hh in range(ML_HEADS):
        cols = slice(hh * ML_DV, (hh + 1) * ML_DV)
        uh = uc[:, cols]
        q = _dot(uh, wq_ref[hh]).astype(BF16)
        k = _dot(uh, wk_ref[hh]).astype(BF16)
        v_aug = jnp.concatenate([mv_ref[:, cols], ones_col], axis=1)
        i_col = z[:, hh:hh + 1]
        b_col = z[:, ML_HEADS + hh:ML_HEADS + hh + 1]
        i_row = zt[hh:hh + 1, :]
        b_row = zt[ML_HEADS + hh:ML_HEADS + hh + 1, :]
        m_prev = m_ref[hh, 0:1, 0:1]
        c_prev = c_ref[hh]

        d_log = jnp.where(causal, b_col + (i_row - b_row), -jnp.inf)
        inter_log = b_col + m_prev
        m_t = jnp.maximum(jnp.max(d_log, axis=1, keepdims=True), inter_log)
        w = jnp.exp(d_log - m_t)
        w_inter = jnp.exp(inter_log - m_t)
        p = (_dot_nt(q, k) * w).astype(BF16)
        tot = _dot(p, v_aug) + w_inter * _dot(q, c_prev.astype(BF16))
        num = tot[:, :ML_DV]
        den = tot[:, ML_DV:ML_DV + 1]
        hout = num / jnp.maximum(jnp.abs(den), jnp.exp(-m_t))

        b_last = b_col[ts - 1:ts, :]
        state_log = b_last - b_col + i_col
        m_new = jnp.maximum(b_last + m_prev, jnp.max(state_log, axis=0, keepdims=True))
        w_s = jnp.exp(state_log - m_new)
        decay_prev = jnp.exp(b_last + m_prev - m_new)
        c_ref[hh] = decay_prev * c_prev + _dot_tn((w_s * k.astype(F32)).astype(BF16), v_aug)
        m_ref[hh] = jnp.broadcast_to(m_new, m_ref.shape[1:])

        y = mo_ref[:, cols].astype(F32) * _head_norm(hout, g_ref[:, cols])
        y_ref[:, cols] = y.astype(BF16)


def _mlstm(mu, mv, mo, mif, cw, cb, wq, wk, fb, g, batch, seq, ts):
    t = mu.shape[0]
    per_b = seq // ts
    tri = jnp.tril(jnp.ones((ts, ts), F32)).astype(BF16)
    row = lambda w: pl.BlockSpec((ts, w), lambda b, s: (b * per_b + s, 0))
    consts = [cw, cb, wq, wk, fb, g, tri]
    return pl.pallas_call(
        _mlstm_kernel,
        grid=(batch, per_b),
        in_specs=[row(ML_W), row(ML_W), row(ML_W), row(LANES)] + [_const_spec(a.shape) for a in consts],
        out_specs=row(ML_W),
        out_shape=jax.ShapeDtypeStruct((t, ML_W), BF16),
        scratch_shapes=[pltpu.VMEM((8, ML_W), F32),
                        pltpu.VMEM((ML_HEADS, ML_DQK, 2 * ML_DV), F32),
                        pltpu.VMEM((ML_HEADS, 8, LANES), F32)],
        compiler_params=_cparams(("arbitrary", "arbitrary")),
        name="mlstm",
    )(mu, mv, mo, mif, *consts)


def _merge_kernel(yh_ref, ym_ref, ga_ref, gb_ref, x_ref, modm_ref, modf_ref, g_ref, wbh_ref, wbm_ref, wo_ref,
                  rw_ref, rb_ref, xo_ref, h_ref, cmb_ref):
    d = x_ref.shape[-1]
    merged = (ga_ref[...].astype(F32) * _dot(yh_ref[...], wbh_ref[...])
              + gb_ref[...].astype(F32) * _dot(ym_ref[...], wbm_ref[...]))
    mix = _dot(merged.astype(BF16), wo_ref[...])
    x_new = x_ref[...] + modm_ref[:, 2 * d:] * mix
    xo_ref[...] = x_new
    modf = modf_ref[...]
    h = _norm_modulate(x_new, g_ref[...], modf[:, :d], modf[:, d:2 * d]).astype(BF16)
    h_ref[...] = h
    logits = _dot(h, rw_ref[...]) + rb_ref[...]
    lane = lax.broadcasted_iota(jnp.int32, logits.shape, 1)
    logits = jnp.where(lane < N_EXPERTS, logits, -jnp.inf)
    m1 = jnp.max(logits, axis=1, keepdims=True)
    i1 = jnp.min(jnp.where(logits == m1, lane, LANES), axis=1, keepdims=True)
    rest = jnp.where(lane == i1, -jnp.inf, logits)
    m2 = jnp.max(rest, axis=1, keepdims=True)
    i2 = jnp.min(jnp.where(rest == m2, lane, LANES), axis=1, keepdims=True)
    e = jnp.exp(m2 - m1)
    g1 = 1.0 / (1.0 + e)
    cmb_ref[...] = jnp.where(lane == i1, g1, 0.0) + jnp.where(lane == i2, e * g1, 0.0)


def _merge(yh, ym, ga, gb, x2, modm, modf, g, wbh, wbm, wo, rw, rb, seq, tm):
    t, d = x2.shape
    per_b = seq // tm
    row = lambda w: pl.BlockSpec((tm, w), lambda i: (i, 0))
    mod_spec = pl.BlockSpec((None, 1, modm.shape[-1]), lambda i: (i // per_b, 0, 0))
    consts = [g, wbh, wbm, wo, rw, rb]
    return pl.pallas_call(
        _merge_kernel,
        grid=(t // tm,),
        in_specs=[row(HG_W), row(ML_W), row(d), row(d), row(d), mod_spec, mod_spec]
                 + [_const_spec(a.shape) for a in consts],
        out_specs=[row(d), row(d), row(LANES)],
        out_shape=[jax.ShapeDtypeStruct((t, d), F32), jax.ShapeDtypeStruct((t, d), BF16),
                   jax.ShapeDtypeStruct((t, LANES), F32)],
        compiler_params=_cparams(("arbitrary",)),
        name="merge_out_proj",
    )(yh, ym, ga, gb, x2, modm, modf, *consts)


def _ffn_kernel(x_ref, h_ref, cmb_ref, mod_ref, w1_ref, w3_ref, w2_ref, fg_ref, o_ref, acc_ref,
                *, routed, final_norm):
    e = pl.program_id(1)
    f = pl.program_id(2)
    d = x_ref.shape[-1]

    @pl.when((e == 0) & (f == 0))
    def _():
        acc_ref[...] = jnp.zeros_like(acc_ref)

    h = h_ref[...]
    act = (_silu(_dot(h, w1_ref[...])) * _dot(h, w3_ref[...])).astype(BF16)
    y = _dot(act, w2_ref[...])
    if routed:
        cmb = cmb_ref[...]
        lane = lax.broadcasted_iota(jnp.int32, cmb.shape, 1)
        y = y * jnp.sum(jnp.where(lane == e, cmb, 0.0), axis=1, keepdims=True)
    acc_ref[...] += y

    @pl.when((e == pl.num_programs(1) - 1) & (f == pl.num_programs(2) - 1))
    def _():
        out = x_ref[...] + mod_ref[:, 2 * d:] * acc_ref[...]
        if final_norm:
            out = out * lax.rsqrt(jnp.mean(out * out, axis=-1, keepdims=True) + EPS) * fg_ref[...]
        o_ref[...] = out


def _ffn(x2, h, cmb, mod, w1, w3, w2, fg, seq, tm, tf, routed, final_norm):
    t, d = x2.shape
    ne, _, dff = w1.shape
    per_b = seq // tm
    row = lambda w: pl.BlockSpec((tm, w), lambda i, e, f: (i, 0))
    return pl.pallas_call(
        functools.partial(_ffn_kernel, routed=routed, final_norm=final_norm),
        grid=(t // tm, ne, dff // tf),
        in_specs=[row(d), row(d), row(LANES),
                  pl.BlockSpec((None, 1, mod.shape[-1]), lambda i, e, f: (i // per_b, 0, 0)),
                  pl.BlockSpec((None, d, tf), lambda i, e, f: (e, 0, f)),
                  pl.BlockSpec((None, d, tf), lambda i, e, f: (e, 0, f)),
                  pl.BlockSpec((None, tf, d), lambda i, e, f: (e, f, 0)),
                  pl.BlockSpec(fg.shape, lambda i, e, f: (0, 0))],
        out_specs=row(d),
        out_shape=jax.ShapeDtypeStruct((t, d), F32),
        scratch_shapes=[pltpu.VMEM((tm, d), F32)],
        compiler_params=_cparams(("arbitrary", "arbitrary", "arbitrary")),
        name="swiglu_routed" if routed else "swiglu_dense",
    )(x2, h, cmb, mod, w1, w3, w2, fg)


def _pick(n, pref):
    while n % pref:
        pref //= 2
    return pref


@jax.jit
def kernel(x, c, norm_mix_g, ada_mix_w, ada_mix_b, w_in, hg_lb_logits, hg_norm_g, ml_conv_w, ml_conv_b, ml_wq,
           ml_wk, ml_fbias, ml_norm_g, w_br_hg, w_br_ml, w_out, norm_ffn_g, ada_ffn_w, ada_ffn_b, ffn_w1,
           ffn_w3, ffn_w2, moe_router_w, moe_router_b, moe_w1, moe_w3, moe_w2, final_norm_g):
    batch, seq, d = x.shape
    depth = w_in.shape[0]
    t = batch * seq
    tm = _pick(seq, 512)
    ts_hg = _pick(seq, 512)
    ts_ml = _pick(seq, 256)

    mod_mix, mod_ffn = _ada_params(c, ada_mix_w, ada_mix_b, ada_ffn_w, ada_ffn_b)
    mod_mix = mod_mix.reshape(depth, batch, 1, 3 * d)
    mod_ffn = mod_ffn.reshape(depth, batch, 1, 3 * d)

    lb_all = jnp.cumsum(jax.nn.softmax(hg_lb_logits.astype(F32), axis=0), axis=0)
    lb_all = lb_all - lb_all[0:1]

    n_main = 4 * HG_W + 3 * ML_W
    x2 = x.reshape(t, d)
    cmb_dummy = jnp.zeros((t, LANES), F32)
    for l in range(depth):
        wl = w_in[l]
        wm = wl[:, :n_main].astype(BF16)
        wif = jnp.pad(wl[:, n_main:n_main + 2 * ML_HEADS], ((0, 0), (0, LANES - 2 * ML_HEADS))).astype(BF16)
        wg = wl[:, n_main + 2 * ML_HEADS:].astype(BF16)
        lb = lb_all[l][None, :]
        lbp = jnp.concatenate([jnp.log(lb), jnp.log1p(-lb), 1.0 - lb, jnp.zeros((5, HG_W), F32)], axis=0)
        q, k, v, gs, lf, mu, mv, mo, ga, gb, mif = _proj(
            x2, mod_mix[l], norm_mix_g[l][None, :], wm, wg, wif, lbp, seq, tm)

        yh = _hgrn2(q, k, v, lf, gs, hg_norm_g[l][None, :], batch, seq, ts_hg)

        fb = jnp.zeros((1, LANES), F32).at[0, ML_HEADS:2 * ML_HEADS].set(ml_fbias[l])
        ym = _mlstm(mu, mv, mo, mif, ml_conv_w[l], ml_conv_b[l][None, :],
                    (ml_wq[l] * (ML_DQK ** -0.5)).astype(BF16), ml_wk[l].astype(BF16), fb,
                    ml_norm_g[l][None, :], batch, seq, ts_ml)

        j = l // 2
        routed = l % 2 == 1
        if routed:
            rw = jnp.pad(moe_router_w[j], ((0, 0), (0, LANES - N_EXPERTS))).astype(BF16)
            rb = jnp.pad(moe_router_b[j], (0, LANES - N_EXPERTS))[None, :]
        else:
            rw = jnp.zeros((d, LANES), BF16)
            rb = jnp.zeros((1, LANES), F32)
        x2, h2, cmb = _merge(yh, ym, ga, gb, x2, mod_mix[l], mod_ffn[l], norm_ffn_g[l][None, :],
                             w_br_hg[l].astype(BF16), w_br_ml[l].astype(BF16), w_out[l].astype(BF16),
                             rw, rb, seq, tm)

        final = l == depth - 1
        fg = final_norm_g[None, :]
        if routed:
            w1, w3, w2 = moe_w1[j].astype(BF16), moe_w3[j].astype(BF16), moe_w2[j].astype(BF16)
        else:
            w1, w3, w2 = ffn_w1[j][None].astype(BF16), ffn_w3[j][None].astype(BF16), ffn_w2[j][None].astype(BF16)
            cmb = cmb_dummy
        x2 = _ffn(x2, h2, cmb, mod_ffn[l], w1, w3, w2, fg, seq, tm, FF_TILE, routed, final)
    return x2.reshape(batch, seq, d)
```

```python
import functools

import jax
import jax.numpy as jnp
from jax import lax
from jax.experimental import pallas as pl
from jax.experimental.pallas import tpu as pltpu

F32 = jnp.float32
BF16 = jnp.bfloat16

EPS = 1e-6
LANES = 128
HG_HEADS = 4
HG_D = 128
HG_W = HG_HEADS * HG_D
ML_HEADS = 4
ML_DV = 128
ML_DQK = 64
ML_W = ML_HEADS * ML_DV
CONV_K = 4
N_EXPERTS = 8
HG_CHUNK = 64
HG_SUB = 16
FF_TILE = 1408
VMEM_LIMIT = 56 * 1024 * 1024


def _cparams(sem):
    return pltpu.CompilerParams(dimension_semantics=sem, vmem_limit_bytes=VMEM_LIMIT)


def _const_spec(shape):
    nd = len(shape)
    return pl.BlockSpec(shape, lambda *_: (0,) * nd, pipeline_mode=pl.Buffered(1))


def _silu(v):
    return v * jax.nn.sigmoid(v)


def _log_sigmoid(z):
    return jnp.minimum(z, 0.0) - jnp.log1p(jnp.exp(-jnp.abs(z)))


def _split_bf16(v):
    hi = v.astype(BF16)
    lo = (v - hi.astype(F32)).astype(BF16)
    return hi, lo


def _dot(a, b):
    return jnp.dot(a, b, preferred_element_type=F32)


def _dot_nt(a, b):
    return lax.dot_general(a, b, (((1,), (1,)), ((), ())), preferred_element_type=F32)


def _dot_tn(a, b):
    return lax.dot_general(a, b, (((0,), (0,)), ((), ())), preferred_element_type=F32)


def _ada_kernel(c_ref, wm_ref, bm_ref, wf_ref, bf_ref, om_ref, of_ref):
    c = c_ref[...]
    ca = _silu(c).astype(BF16)
    om_ref[...] = _dot(ca, wm_ref[...].astype(BF16)) + bm_ref[...]
    of_ref[...] = _dot(ca, wf_ref[...].astype(BF16)) + bf_ref[...]


def _ada_params(c, wm, bm, wf, bf):
    depth, d, d3 = wm.shape
    b = c.shape[0]
    nt = d3 // d
    w_spec = pl.BlockSpec((None, d, d), lambda l, j: (l, 0, j))
    b_spec = pl.BlockSpec((None, 1, d), lambda l, j: (l, 0, j))
    o_spec = pl.BlockSpec((None, b, d), lambda l, j: (l, 0, j))
    out = jax.ShapeDtypeStruct((depth, b, d3), F32)
    return pl.pallas_call(
        _ada_kernel,
        grid=(depth, nt),
        in_specs=[pl.BlockSpec((b, d), lambda l, j: (0, 0)), w_spec, b_spec, w_spec, b_spec],
        out_specs=[o_spec, o_spec],
        out_shape=[out, out],
        compiler_params=_cparams(("arbitrary", "arbitrary")),
        name="ada_params",
    )(c, wm, bm.reshape(depth, 1, d3), wf, bf.reshape(depth, 1, d3))


def _norm_modulate(x, g, shift, scale):
    ms = jnp.mean(x * x, axis=-1, keepdims=True)
    y = x * lax.rsqrt(ms + EPS) * g
    return y * (1.0 + scale) + shift


def _proj_kernel(x_ref, mod_ref, g_ref, wm_ref, wg_ref, wif_ref, lbp_ref,
                 q_ref, k_ref, v_ref, gs_ref, lf_ref, mu_ref, mv_ref, mo_ref, ga_ref, gb_ref, mif_ref):
    d = x_ref.shape[-1]
    mod = mod_ref[...]
    h = _norm_modulate(x_ref[...], g_ref[...], mod[:, :d], mod[:, d:2 * d]).astype(BF16)

    def piece(w_ref, j, width=HG_W):
        return _dot(h, w_ref[:, j * width:(j + 1) * width])

    q_ref[...] = _silu(piece(wm_ref, 0)).astype(BF16)
    zf = piece(wm_ref, 1)
    log_lb, log1m_lb, one_m_lb = lbp_ref[0:1, :], lbp_ref[1:2, :], lbp_ref[2:3, :]
    t = log1m_lb + _log_sigmoid(zf)
    lf_ref[...] = jnp.maximum(log_lb, t) + jnp.log1p(jnp.exp(-jnp.abs(log_lb - t)))
    k_ref[...] = (one_m_lb * jax.nn.sigmoid(-zf)).astype(BF16)
    v_ref[...] = piece(wm_ref, 2).astype(BF16)
    gs_ref[...] = _silu(piece(wm_ref, 3)).astype(BF16)
    mu_ref[...] = piece(wm_ref, 4).astype(BF16)
    mv_ref[...] = piece(wm_ref, 5).astype(BF16)
    mo_ref[...] = jax.nn.sigmoid(piece(wm_ref, 6)).astype(BF16)
    ga_ref[...] = jax.nn.sigmoid(piece(wg_ref, 0, d)).astype(BF16)
    gb_ref[...] = jax.nn.sigmoid(piece(wg_ref, 1, d)).astype(BF16)
    mif_ref[...] = _dot(h, wif_ref[...])


def _proj(x2, mod, g, wm, wg, wif, lbp, seq, tm):
    t, d = x2.shape
    per_b = seq // tm
    row = lambda w: pl.BlockSpec((tm, w), lambda i: (i, 0))
    sds = lambda w, dt: jax.ShapeDtypeStruct((t, w), dt)
    outs = [(HG_W, BF16)] * 4 + [(HG_W, F32)] + [(ML_W, BF16)] * 3 + [(d, BF16)] * 2 + [(LANES, F32)]
    return pl.pallas_call(
        _proj_kernel,
        grid=(t // tm,),
        in_specs=[row(d),
                  pl.BlockSpec((None, 1, mod.shape[-1]), lambda i: (i // per_b, 0, 0)),
                  _const_spec(g.shape), _const_spec(wm.shape), _const_spec(wg.shape),
                  _const_spec(wif.shape), _const_spec(lbp.shape)],
        out_specs=[row(w) for w, _ in outs],
        out_shape=[sds(w, dt) for w, dt in outs],
        compiler_params=_cparams(("arbitrary",)),
        name="in_proj",
    )(x2, mod, g, wm, wg, wif, lbp)


def _head_norm(o, g):
    return o * lax.rsqrt(jnp.mean(o * o, axis=-1, keepdims=True) + EPS) * g


def _hgrn2_kernel(q_ref, k_ref, v_ref, lf_ref, gs_ref, g_ref, tri_ref, sel_ref, y_ref, st_ref):
    ts = q_ref.shape[0]
    nsub = HG_CHUNK // HG_SUB

    @pl.when(pl.program_id(1) == 0)
    def _():
        st_ref[...] = jnp.zeros_like(st_ref)

    tri = tri_ref[...]
    row = lax.broadcasted_iota(jnp.int32, (HG_CHUNK, HG_CHUNK), 0)
    col = lax.broadcasted_iota(jnp.int32, (HG_CHUNK, HG_CHUNK), 1)
    diag_mask = (row // HG_SUB == col // HG_SUB) & (col <= row)

    def chunk(c, carry):
        r0 = pl.multiple_of(c * HG_CHUNK, HG_CHUNK)
        rows = pl.ds(r0, HG_CHUNK)
        lf = lf_ref[rows, :]
        hi, lo = _split_bf16(lf)
        a_all = _dot(tri, hi) + _dot(tri, lo)
        p_heads = []
        per_head = []
        for hh in range(HG_HEADS):
            cols = slice(hh * HG_D, (hh + 1) * HG_D)
            a = a_all[:, cols]
            qf = q_ref[rows, cols].astype(F32)
            kf = k_ref[rows, cols].astype(F32)
            a4 = a.reshape(nsub, HG_SUB, HG_D)
            q4 = qf.reshape(nsub, HG_SUB, HG_D)
            k4 = kf.reshape(nsub, HG_SUB, HG_D)
            pieces = []
            for s in range(HG_SUB):
                dec = jnp.exp(jnp.minimum(a4 - a4[:, s:s + 1, :], 0.0))
                pieces.append((q4 * k4[:, s:s + 1, :] * dec).reshape(HG_CHUNK, HG_D).astype(BF16))
            p_heads.append(jnp.concatenate(pieces, axis=1))
            per_head.append((a, qf, kf, cols))
        diag = _dot(jnp.concatenate(p_heads, axis=0), sel_ref[...])
        for hh, (a, qf, kf, cols) in enumerate(per_head):
            vb = v_ref[rows, cols]
            a_last = a[HG_CHUNK - 1:HG_CHUNK, :]
            qs, ks = [], []
            for j in range(nsub - 1):
                e = a[(j + 1) * HG_SUB - 1:(j + 1) * HG_SUB, :]
                lo_r, hi_r = j * HG_SUB, (j + 1) * HG_SUB
                kt = kf[lo_r:hi_r] * jnp.exp(e - a[lo_r:hi_r])
                qt = qf[hi_r:] * jnp.exp(a[hi_r:] - e)
                ks.append(jnp.concatenate(
                    [jnp.zeros((lo_r, HG_D), F32), kt, jnp.zeros((HG_CHUNK - hi_r, HG_D), F32)], axis=0)
                    if lo_r else jnp.concatenate([kt, jnp.zeros((HG_CHUNK - hi_r, HG_D), F32)], axis=0))
                qs.append(jnp.concatenate([jnp.zeros((hi_r, HG_D), F32), qt], axis=0))
            s_off = _dot_nt(jnp.concatenate(qs, axis=1).astype(BF16), jnp.concatenate(ks, axis=1).astype(BF16))
            s_diag = diag[hh * HG_CHUNK:(hh + 1) * HG_CHUNK, :HG_CHUNK]
            s_full = s_off + jnp.where(diag_mask, s_diag, 0.0)
            st = st_ref[hh]
            o = _dot(s_full.astype(BF16), vb) + _dot_nt((qf * jnp.exp(a)).astype(BF16), st.astype(BF16))
            k_dec = (kf * jnp.exp(a_last - a)).astype(BF16)
            st_ref[hh] = st * jnp.exp(a_last) + _dot_tn(vb, k_dec)
            y = _head_norm(o, g_ref[:, cols]) * gs_ref[rows, cols].astype(F32)
            y_ref[rows, cols] = y.astype(BF16)
        return carry

    lax.fori_loop(0, ts // HG_CHUNK, chunk, 0)


def _hgrn2(q, k, v, lf, gs, g, batch, seq, ts):
    t = q.shape[0]
    per_b = seq // ts
    tri = jnp.tril(jnp.ones((HG_CHUNK, HG_CHUNK), F32)).astype(BF16)
    r = jnp.arange(HG_SUB * HG_D)[:, None] // HG_D
    cidx = jnp.arange(LANES)[None, :]
    sel = ((cidx % HG_SUB == r) & (cidx < HG_CHUNK)).astype(BF16)
    row = pl.BlockSpec((ts, HG_W), lambda b, s: (b * per_b + s, 0))
    return pl.pallas_call(
        _hgrn2_kernel,
        grid=(batch, per_b),
        in_specs=[row, row, row, row, row, _const_spec(g.shape), _const_spec(tri.shape), _const_spec(sel.shape)],
        out_specs=row,
        out_shape=jax.ShapeDtypeStruct((t, HG_W), BF16),
        scratch_shapes=[pltpu.VMEM((HG_HEADS, HG_D, HG_D), F32)],
        compiler_params=_cparams(("arbitrary", "arbitrary")),
        name="hgrn2",
    )(q, k, v, lf, gs, g, tri, sel)


def _mlstm_kernel(mu_ref, mv_ref, mo_ref, mif_ref, cw_ref, cb_ref, wq_ref, wk_ref, fb_ref, g_ref, tri_ref,
                  y_ref, carry_ref, c_ref, m_ref):
    ts = mu_ref.shape[0]

    @pl.when(pl.program_id(1) == 0)
    def _():
        carry_ref[...] = jnp.zeros_like(carry_ref)
        c_ref[...] = jnp.zeros_like(c_ref)
        m_ref[...] = jnp.zeros_like(m_ref)

    u = mu_ref[...].astype(F32)
    ext = jnp.concatenate([carry_ref[...], u], axis=0)
    carry_ref[...] = u[ts - 8:, :]
    conv = cb_ref[...]
    for j in range(CONV_K):
        off = 8 - (CONV_K - 1) + j
        conv = conv + cw_ref[j:j + 1, :] * ext[off:off + ts, :]
    uc = _silu(conv).astype(BF16)

    mif = mif_ref[...]
    lf = _log_sigmoid(mif + fb_ref[...])
    hi, lo = _split_bf16(lf)
    tri = tri_ref[...]
    bcum = _dot(tri, hi) + _dot(tri, lo)
    lane = lax.broadcasted_iota(jnp.int32, mif.shape, 1)
    z = jnp.where(lane < ML_HEADS, mif, bcum)
    zt = z.T

    row = lax.broadcasted_iota(jnp.int32, (ts, ts), 0)
    col = lax.broadcasted_iota(jnp.int32, (ts, ts), 1)
    causal = col <= row
    ones_col = (lax.broadcasted_iota(jnp.int32, (ts, ML_DV), 1) == 0).astype(BF16)

    for hh in range(ML_HEADS):
        cols = slice(hh * ML_DV, (hh + 1) * ML_DV)
        uh = uc[:, cols]
        q = _dot(uh, wq_ref[hh]).astype(BF16)
        k = _dot(uh, wk_ref[hh]).astype(BF16)
        v_aug = jnp.concatenate([mv_ref[:, cols], ones_col], axis=1)
        i_col = z[:, hh:hh + 1]
        b_col = z[:, ML_HEADS + hh:ML_HEADS + hh + 1]
        i_row = zt[hh:hh + 1, :]
        b_row = zt[ML_HEADS + hh:ML_HEADS + hh + 1, :]
        m_prev = m_ref[hh, 0:1, 0:1]
        c_prev = c_ref[hh]

        d_log = jnp.where(causal, b_col + (i_row - b_row), -jnp.inf)
        inter_log = b_col + m_prev
        m_t = jnp.maximum(jnp.max(d_log, axis=1, keepdims=True), inter_log)
        w = jnp.exp(d_log - m_t)
        w_inter = jnp.exp(inter_log - m_t)
        p = (_dot_nt(q, k) * w).astype(BF16)
        tot = _dot(p, v_aug) + w_inter * _dot(q, c_prev.astype(BF16))
        num = tot[:, :ML_DV]
        den = tot[:, ML_DV:ML_DV + 1]
        hout = num / jnp.maximum(jnp.abs(den), jnp.exp(-m_t))

        b_last = b_col[ts - 1:ts, :]
        state_log = b_last - b_col + i_col
        m_new = jnp.maximum(b_last + m_prev, jnp.max(state_log, axis=0, keepdims=True))
        w_s = jnp.exp(state_log - m_new)
        decay_prev = jnp.exp(b_last + m_prev - m_new)
        c_ref[hh] = decay_prev * c_prev + _dot_tn((w_s * k.astype(F32)).astype(BF16), v_aug)
        m_ref[hh] = jnp.broadcast_to(m_new, m_ref.shape[1:])

        y = mo_ref[:, cols].astype(F32) * _head_norm(hout, g_ref[:, cols])
        y_ref[:, cols] = y.astype(BF16)


def _mlstm(mu, mv, mo, mif, cw, cb, wq, wk, fb, g, batch, seq, ts):
    t = mu.shape[0]
    per_b = seq // ts
    tri = jnp.tril(jnp.ones((ts, ts), F32)).astype(BF16)
    row = lambda w: pl.BlockSpec((ts, w), lambda b, s: (b * per_b + s, 0))
    consts = [cw, cb, wq, wk, fb, g, tri]
    return pl.pallas_call(
        _mlstm_kernel,
        grid=(batch, per_b),
        in_specs=[row(ML_W), row(ML_W), row(ML_W), row(LANES)] + [_const_spec(a.shape) for a in consts],
        out_specs=row(ML_W),
        out_shape=jax.ShapeDtypeStruct((t, ML_W), BF16),
        scratch_shapes=[pltpu.VMEM((8, ML_W), F32),
                        pltpu.VMEM((ML_HEADS, ML_DQK, 2 * ML_DV), F32),
                        pltpu.VMEM((ML_HEADS, 8, LANES), F32)],
        compiler_params=_cparams(("arbitrary", "arbitrary")),
        name="mlstm",
    )(mu, mv, mo, mif, *consts)


def _merge_kernel(yh_ref, ym_ref, ga_ref, gb_ref, x_ref, modm_ref, modf_ref, g_ref, wbh_ref, wbm_ref, wo_ref,
                  *rest, routed):
    d = x_ref.shape[-1]
    merged = (ga_ref[...].astype(F32) * _dot(yh_ref[...], wbh_ref[...])
              + gb_ref[...].astype(F32) * _dot(ym_ref[...], wbm_ref[...]))
    mix = _dot(merged.astype(BF16), wo_ref[...])
    x_new = x_ref[...] + modm_ref[:, 2 * d:] * mix
    modf = modf_ref[...]
    h = _norm_modulate(x_new, g_ref[...], modf[:, :d], modf[:, d:2 * d])
    if not routed:
        xo_ref, h_ref = rest
        xo_ref[...] = x_new
        h_ref[...] = h.astype(BF16)
        return
    rw_ref, rb_ref, xo_ref, h_ref, ridx_ref, gts_ref = rest
    xo_ref[...] = x_new
    h_ref[...] = h
    logits = _dot(h.astype(BF16), rw_ref[...]) + rb_ref[...]
    lane = lax.broadcasted_iota(jnp.int32, logits.shape, 1)
    logits = jnp.where(lane < N_EXPERTS, logits, -jnp.inf)
    m1 = jnp.max(logits, axis=1, keepdims=True)
    i1 = jnp.min(jnp.where(logits == m1, lane, LANES), axis=1, keepdims=True)
    others = jnp.where(lane == i1, -jnp.inf, logits)
    m2 = jnp.max(others, axis=1, keepdims=True)
    i2 = jnp.min(jnp.where(others == m2, lane, LANES), axis=1, keepdims=True)
    e = jnp.exp(m2 - m1)
    g1 = 1.0 / (1.0 + e)
    ridx_ref[...] = jnp.where(lane == 0, i1, jnp.where(lane == 1, i2, 0))
    gts_ref[...] = jnp.where(lane == 0, g1, jnp.where(lane == 1, e * g1, 0.0))


def _merge(yh, ym, ga, gb, x2, modm, modf, g, wbh, wbm, wo, router, seq, tm):
    t, d = x2.shape
    per_b = seq // tm
    routed = router is not None
    row = lambda w: pl.BlockSpec((tm, w), lambda i: (i, 0))
    mod_spec = pl.BlockSpec((None, 1, modm.shape[-1]), lambda i: (i // per_b, 0, 0))
    consts = [g, wbh, wbm, wo] + (list(router) if routed else [])
    outs = [(d, F32), (d, F32 if routed else BF16)] + ([(LANES, jnp.int32), (LANES, F32)] if routed else [])
    return pl.pallas_call(
        functools.partial(_merge_kernel, routed=routed),
        grid=(t // tm,),
        in_specs=[row(HG_W), row(ML_W), row(d), row(d), row(d), mod_spec, mod_spec]
                 + [_const_spec(a.shape) for a in consts],
        out_specs=[row(w) for w, _ in outs],
        out_shape=[jax.ShapeDtypeStruct((t, w), dt) for w, dt in outs],
        compiler_params=_cparams(("arbitrary",)),
        name="merge_out_proj_routed" if routed else "merge_out_proj",
    )(yh, ym, ga, gb, x2, modm, modf, *consts)


def _final_norm(out, fg_ref):
    return out * lax.rsqrt(jnp.mean(out * out, axis=-1, keepdims=True) + EPS) * fg_ref[...]


def _ffn_kernel(x_ref, h_ref, mod_ref, w1_ref, w3_ref, w2_ref, fg_ref, o_ref, acc_ref, *, final_norm):
    f = pl.program_id(1)
    d = x_ref.shape[-1]

    @pl.when(f == 0)
    def _():
        acc_ref[...] = jnp.zeros_like(acc_ref)

    h = h_ref[...]
    act = (_silu(_dot(h, w1_ref[...])) * _dot(h, w3_ref[...])).astype(BF16)
    acc_ref[...] += _dot(act, w2_ref[...])

    @pl.when(f == pl.num_programs(1) - 1)
    def _():
        out = x_ref[...] + mod_ref[:, 2 * d:] * acc_ref[...]
        o_ref[...] = _final_norm(out, fg_ref) if final_norm else out


def _ffn(x2, h, mod, w1, w3, w2, fg, seq, tm, tf, final_norm):
    t, d = x2.shape
    dff = w1.shape[-1]
    per_b = seq // tm
    row = lambda w: pl.BlockSpec((tm, w), lambda i, f: (i, 0))
    return pl.pallas_call(
        functools.partial(_ffn_kernel, final_norm=final_norm),
        grid=(t // tm, dff // tf),
        in_specs=[row(d), row(d),
                  pl.BlockSpec((None, 1, mod.shape[-1]), lambda i, f: (i // per_b, 0, 0)),
                  pl.BlockSpec((d, tf), lambda i, f: (0, f)),
                  pl.BlockSpec((d, tf), lambda i, f: (0, f)),
                  pl.BlockSpec((tf, d), lambda i, f: (f, 0)),
                  pl.BlockSpec(fg.shape, lambda i, f: (0, 0))],
        out_specs=row(d),
        out_shape=jax.ShapeDtypeStruct((t, d), F32),
        scratch_shapes=[pltpu.VMEM((tm, d), F32)],
        compiler_params=_cparams(("arbitrary", "arbitrary")),
        name="swiglu_dense",
    )(x2, h, mod, w1, w3, w2, fg)


def _route_plan(ridx, t, tm):
    n_assign = 2 * t
    n_tiles = n_assign // tm + N_EXPERTS
    e_flat = jnp.concatenate([ridx[:, 0], ridx[:, 1]])
    order = jnp.sort(e_flat * n_assign + jnp.arange(n_assign, dtype=jnp.int32)) % n_assign
    counts = jnp.sum((e_flat[:, None] == jnp.arange(N_EXPERTS, dtype=jnp.int32)[None, :]).astype(jnp.int32), axis=0)
    tiles_e = (counts + tm - 1) // tm
    tile_end = jnp.cumsum(tiles_e)
    n_used = tile_end[-1]
    seg_start = jnp.cumsum(counts) - counts
    ti = jnp.arange(n_tiles, dtype=jnp.int32)
    te = jnp.minimum(jnp.sum((ti[:, None] >= tile_end[None, :]).astype(jnp.int32), axis=1), N_EXPERTS - 1)
    j = ti - (tile_end - tiles_e)[te]
    n_valid = jnp.where(ti < n_used, jnp.clip(counts[te] - j * tm, 0, tm), 0)
    r = jnp.arange(tm, dtype=jnp.int32)[None, :]
    valid = r < n_valid[:, None]
    a = order[jnp.clip(seg_start[te][:, None] + j[:, None] * tm + r, 0, n_assign - 1)]
    gather_tok = jnp.where(valid, a % t, 0)
    scatter_row = jnp.where(valid, a, n_assign + (ti % 2)[:, None] * tm + r)
    tile_expert = jnp.where(ti < n_used, te, te[jnp.maximum(n_used - 1, 0)])
    return (gather_tok.reshape(n_tiles, 1, tm), scatter_row.reshape(n_tiles, 1, tm),
            tile_expert.astype(jnp.int32), n_used.reshape(1).astype(jnp.int32))


def _moe_kernel(te_ref, nu_ref, gcur_ref, gnext_ref, scur_ref, h_hbm, w1_ref, w3_ref, w2_ref, y_hbm,
                xbuf, ybuf, gsem, ssem):
    i = pl.program_id(0)
    n_used = nu_ref[0]
    tm = xbuf.shape[1]
    slot = i % 2

    def gather_row(idx_ref, sl, r):
        return pltpu.make_async_copy(h_hbm.at[pl.ds(idx_ref[0, r], 1)], xbuf.at[sl, pl.ds(r, 1)], gsem.at[sl])

    def scatter_row(sl, r):
        return pltpu.make_async_copy(ybuf.at[sl, pl.ds(r, 1)], y_hbm.at[pl.ds(scur_ref[0, r], 1)], ssem.at[sl])

    def start_gather(idx_ref, sl):
        def body(r, carry):
            gather_row(idx_ref, sl, r).start()
            return carry
        lax.fori_loop(0, tm, body, 0, unroll=8)

    def wait_gather(sl):
        pltpu.make_async_copy(h_hbm.at[pl.ds(0, tm)], xbuf.at[sl], gsem.at[sl]).wait()

    def wait_scatter(sl):
        pltpu.make_async_copy(ybuf.at[sl], y_hbm.at[pl.ds(0, tm)], ssem.at[sl]).wait()

    @pl.when(i == 0)
    def _():
        start_gather(gcur_ref, 0)
        n_assign = y_hbm.shape[0] - 2 * tm
        ybuf[1] = jnp.zeros(ybuf.shape[1:], F32)
        for half in range(2):
            fill = pltpu.make_async_copy(ybuf.at[1], y_hbm.at[pl.ds(n_assign + half * tm, tm)], ssem.at[1])
            fill.start()
            fill.wait()

    @pl.when(i < n_used)
    def _():
        wait_gather(slot)

        @pl.when(i + 1 < n_used)
        def _():
            start_gather(gnext_ref, 1 - slot)

        @pl.when(i >= 2)
        def _():
            wait_scatter(slot)

        x = xbuf[slot].astype(BF16)
        act = (_silu(_dot(x, w1_ref[...])) * _dot(x, w3_ref[...])).astype(BF16)
        ybuf[slot] = _dot(act, w2_ref[...])

        def body(r, carry):
            scatter_row(slot, r).start()
            return carry
        lax.fori_loop(0, tm, body, 0, unroll=8)

        @pl.when(i == n_used - 1)
        def _():
            wait_scatter(slot)

            @pl.when(i >= 1)
            def _():
                wait_scatter(1 - slot)


def _moe(h, plan, w1, w3, w2, tm):
    t, d = h.shape
    gather_tok, scatter_row, tile_expert, n_used = plan
    n_tiles = gather_tok.shape[0]
    dff = w1.shape[-1]
    idx_spec = lambda nxt: pl.BlockSpec(
        (None, 1, tm), lambda i, te, nu: (jnp.minimum(i + nxt, n_tiles - 1), 0, 0), memory_space=pltpu.SMEM)
    grid_spec = pltpu.PrefetchScalarGridSpec(
        num_scalar_prefetch=2,
        grid=(n_tiles,),
        in_specs=[idx_spec(0), idx_spec(1), idx_spec(0),
                  pl.BlockSpec(memory_space=pl.ANY),
                  pl.BlockSpec((None, d, dff), lambda i, te, nu: (te[i], 0, 0)),
                  pl.BlockSpec((None, d, dff), lambda i, te, nu: (te[i], 0, 0)),
                  pl.BlockSpec((None, dff, d), lambda i, te, nu: (te[i], 0, 0))],
        out_specs=pl.BlockSpec(memory_space=pl.ANY),
        scratch_shapes=[pltpu.VMEM((2, tm, d), F32), pltpu.VMEM((2, tm, d), F32),
                        pltpu.SemaphoreType.DMA((2,)), pltpu.SemaphoreType.DMA((2,))],
    )
    return pl.pallas_call(
        _moe_kernel,
        grid_spec=grid_spec,
        out_shape=jax.ShapeDtypeStruct((2 * t + 2 * tm, d), F32),
        compiler_params=_cparams(("arbitrary",)),
        name="moe_experts",
    )(tile_expert, n_used, gather_tok, gather_tok, scatter_row, h, w1, w3, w2)


def _combine_kernel(x_ref, y0_ref, y1_ref, gts_ref, mod_ref, fg_ref, o_ref, *, final_norm):
    d = x_ref.shape[-1]
    gts = gts_ref[...]
    ff = gts[:, 0:1] * y0_ref[...] + gts[:, 1:2] * y1_ref[...]
    out = x_ref[...] + mod_ref[:, 2 * d:] * ff
    o_ref[...] = _final_norm(out, fg_ref) if final_norm else out


def _combine(x2, y, gts, mod, fg, seq, tm, final_norm):
    t, d = x2.shape
    per_b = seq // tm
    nb = t // tm
    row = lambda w: pl.BlockSpec((tm, w), lambda i: (i, 0))
    return pl.pallas_call(
        functools.partial(_combine_kernel, final_norm=final_norm),
        grid=(nb,),
        in_specs=[row(d), row(d), pl.BlockSpec((tm, d), lambda i: (nb + i, 0)), row(LANES),
                  pl.BlockSpec((None, 1, mod.shape[-1]), lambda i: (i // per_b, 0, 0)),
                  pl.BlockSpec(fg.shape, lambda i: (0, 0))],
        out_specs=row(d),
        out_shape=jax.ShapeDtypeStruct((t, d), F32),
        compiler_params=_cparams(("arbitrary",)),
        name="moe_combine",
    )(x2, y, y, gts, mod, fg)


def _pick(n, pref):
    while n % pref:
        pref //= 2
    return pref


@jax.jit
def kernel(x, c, norm_mix_g, ada_mix_w, ada_mix_b, w_in, hg_lb_logits, hg_norm_g, ml_conv_w, ml_conv_b, ml_wq,
           ml_wk, ml_fbias, ml_norm_g, w_br_hg, w_br_ml, w_out, norm_ffn_g, ada_ffn_w, ada_ffn_b, ffn_w1,
           ffn_w3, ffn_w2, moe_router_w, moe_router_b, moe_w1, moe_w3, moe_w2, final_norm_g):
    batch, seq, d = x.shape
    depth = w_in.shape[0]
    t = batch * seq
    tm = _pick(seq, 512)
    ts_hg = _pick(seq, 512)
    ts_ml = _pick(seq, 256)

    mod_mix, mod_ffn = _ada_params(c, ada_mix_w, ada_mix_b, ada_ffn_w, ada_ffn_b)
    mod_mix = mod_mix.reshape(depth, batch, 1, 3 * d)
    mod_ffn = mod_ffn.reshape(depth, batch, 1, 3 * d)

    lb_all = jnp.cumsum(jax.nn.softmax(hg_lb_logits.astype(F32), axis=0), axis=0)
    lb_all = lb_all - lb_all[0:1]

    n_main = 4 * HG_W + 3 * ML_W
    x2 = x.reshape(t, d)
    for l in range(depth):
        wl = w_in[l]
        wm = wl[:, :n_main].astype(BF16)
        wif = jnp.pad(wl[:, n_main:n_main + 2 * ML_HEADS], ((0, 0), (0, LANES - 2 * ML_HEADS))).astype(BF16)
        wg = wl[:, n_main + 2 * ML_HEADS:].astype(BF16)
        lb = lb_all[l][None, :]
        lbp = jnp.concatenate([jnp.log(lb), jnp.log1p(-lb), 1.0 - lb, jnp.zeros((5, HG_W), F32)], axis=0)
        q, k, v, gs, lf, mu, mv, mo, ga, gb, mif = _proj(
            x2, mod_mix[l], norm_mix_g[l][None, :], wm, wg, wif, lbp, seq, tm)

        yh = _hgrn2(q, k, v, lf, gs, hg_norm_g[l][None, :], batch, seq, ts_hg)

        fb = jnp.zeros((1, LANES), F32).at[0, ML_HEADS:2 * ML_HEADS].set(ml_fbias[l])
        ym = _mlstm(mu, mv, mo, mif, ml_conv_w[l], ml_conv_b[l][None, :],
                    (ml_wq[l] * (ML_DQK ** -0.5)).astype(BF16), ml_wk[l].astype(BF16), fb,
                    ml_norm_g[l][None, :], batch, seq, ts_ml)

        j = l // 2
        routed = l % 2 == 1
        router = None
        if routed:
            router = (jnp.pad(moe_router_w[j], ((0, 0), (0, LANES - N_EXPERTS))).astype(BF16),
                      jnp.pad(moe_router_b[j], (0, LANES - N_EXPERTS))[None, :])
        merged = _merge(yh, ym, ga, gb, x2, mod_mix[l], mod_ffn[l], norm_ffn_g[l][None, :],
                        w_br_hg[l].astype(BF16), w_br_ml[l].astype(BF16), w_out[l].astype(BF16),
                        router, seq, tm)

        final = l == depth - 1
        fg = final_norm_g[None, :]
        if routed:
            x2, h2, ridx, gts = merged
            y = _moe(h2, _route_plan(ridx, t, tm), moe_w1[j].astype(BF16), moe_w3[j].astype(BF16),
                     moe_w2[j].astype(BF16), tm)
            x2 = _combine(x2, y, gts, mod_ffn[l], fg, seq, tm, final)
        else:
            x2, h2 = merged
            x2 = _ffn(x2, h2, mod_ffn[l], ffn_w1[j].astype(BF16), ffn_w3[j].astype(BF16),
                      ffn_w2[j].astype(BF16), fg, seq, tm, FF_TILE, final)
    return x2.reshape(batch, seq, d)
```

```python
import functools

import jax
import jax.numpy as jnp
from jax import lax
from jax.experimental import pallas as pl
from jax.experimental.pallas import tpu as pltpu

F32 = jnp.float32
BF16 = jnp.bfloat16

EPS = 1e-6
LANES = 128
ISSUE_UNROLL = 8
HG_HEADS = 4
HG_D = 128
HG_W = HG_HEADS * HG_D
ML_HEADS = 4
ML_DV = 128
ML_DQK = 64
ML_W = ML_HEADS * ML_DV
CONV_K = 4
N_EXPERTS = 8
HG_CHUNK = 64
HG_SUB = 8
LOG2E = 1.4426950408889634
FF_TILE = 1408
VMEM_LIMIT = 56 * 1024 * 1024


def _cparams(sem):
    return pltpu.CompilerParams(dimension_semantics=sem, vmem_limit_bytes=VMEM_LIMIT)


def _const_spec(shape):
    nd = len(shape)
    return pl.BlockSpec(shape, lambda *_: (0,) * nd, pipeline_mode=pl.Buffered(1))


def _silu(v):
    return v * jax.nn.sigmoid(v)


def _log_sigmoid(z):
    return jnp.minimum(z, 0.0) - jnp.log1p(jnp.exp(-jnp.abs(z)))


def _split_bf16(v):
    hi = v.astype(BF16)
    lo = (v - hi.astype(F32)).astype(BF16)
    return hi, lo


def _dot(a, b):
    return jnp.dot(a, b, preferred_element_type=F32)


def _dot_nt(a, b):
    return lax.dot_general(a, b, (((1,), (1,)), ((), ())), preferred_element_type=F32)


def _dot_tn(a, b):
    return lax.dot_general(a, b, (((0,), (0,)), ((), ())), preferred_element_type=F32)


def _ada_kernel(c_ref, wm_ref, bm_ref, wf_ref, bf_ref, om_ref, of_ref):
    c = c_ref[...]
    ca = _silu(c).astype(BF16)
    om_ref[...] = _dot(ca, wm_ref[...].astype(BF16)) + bm_ref[...]
    of_ref[...] = _dot(ca, wf_ref[...].astype(BF16)) + bf_ref[...]


def _ada_params(c, wm, bm, wf, bf):
    depth, d, d3 = wm.shape
    b = c.shape[0]
    nt = d3 // d
    w_spec = pl.BlockSpec((None, d, d), lambda l, j: (l, 0, j))
    b_spec = pl.BlockSpec((None, 1, d), lambda l, j: (l, 0, j))
    o_spec = pl.BlockSpec((None, b, d), lambda l, j: (l, 0, j))
    out = jax.ShapeDtypeStruct((depth, b, d3), F32)
    return pl.pallas_call(
        _ada_kernel,
        grid=(depth, nt),
        in_specs=[pl.BlockSpec((b, d), lambda l, j: (0, 0)), w_spec, b_spec, w_spec, b_spec],
        out_specs=[o_spec, o_spec],
        out_shape=[out, out],
        compiler_params=_cparams(("arbitrary", "arbitrary")),
        name="ada_params",
    )(c, wm, bm.reshape(depth, 1, d3), wf, bf.reshape(depth, 1, d3))


def _norm_modulate(x, g, shift, scale):
    ms = jnp.mean(x * x, axis=-1, keepdims=True)
    y = x * lax.rsqrt(ms + EPS) * g
    return y * (1.0 + scale) + shift


def _proj_kernel(x_ref, mod_ref, g_ref, wm_ref, wg_ref, wif_ref, lbp_ref, cw_ref, cb_ref, wq_ref, wk_ref,
                 q_ref, k_ref, v_ref, gs_ref, lf_ref, mq_ref, mk_ref, mv_ref, mo_ref, ga_ref, gb_ref, mif_ref,
                 carry_ref, *, per_b):
    tm, d = x_ref.shape
    mod = mod_ref[...]
    h = _norm_modulate(x_ref[...], g_ref[...], mod[:, :d], mod[:, d:2 * d]).astype(BF16)

    def piece(w_ref, j, width=HG_W):
        return _dot(h, w_ref[:, j * width:(j + 1) * width])

    q_ref[...] = _silu(piece(wm_ref, 0)).astype(BF16)
    zf = piece(wm_ref, 1)
    log_lb, log1m_lb, one_m_lb = lbp_ref[0:1, :], lbp_ref[1:2, :], lbp_ref[2:3, :]
    t = log1m_lb + _log_sigmoid(zf)
    lf_ref[...] = jnp.maximum(log_lb, t) + jnp.log1p(jnp.exp(-jnp.abs(log_lb - t)))
    k_ref[...] = (one_m_lb * jax.nn.sigmoid(-zf)).astype(BF16)
    v_ref[...] = piece(wm_ref, 2).astype(BF16)
    gs_ref[...] = _silu(piece(wm_ref, 3)).astype(BF16)
    @pl.when(pl.program_id(0) % per_b == 0)
    def _():
        carry_ref[...] = jnp.zeros_like(carry_ref)

    u = piece(wm_ref, 4)
    ext = jnp.concatenate([carry_ref[...], u], axis=0)
    carry_ref[...] = u[tm - 8:, :]
    conv = cb_ref[...]
    for j in range(CONV_K):
        off = 8 - (CONV_K - 1) + j
        conv = conv + cw_ref[j:j + 1, :] * ext[off:off + tm, :]
    uc = _silu(conv).astype(BF16)
    for hh in range(ML_HEADS):
        cols = slice(hh * ML_DV, (hh + 1) * ML_DV)
        mq_ref[:, cols] = _dot(uc[:, cols], wq_ref[hh]).astype(BF16)
        mk_ref[:, cols] = _dot(uc[:, cols], wk_ref[hh]).astype(BF16)
    mv_ref[...] = piece(wm_ref, 5).astype(BF16)
    mo_ref[...] = jax.nn.sigmoid(piece(wm_ref, 6)).astype(BF16)
    ga_ref[...] = jax.nn.sigmoid(piece(wg_ref, 0, d)).astype(BF16)
    gb_ref[...] = jax.nn.sigmoid(piece(wg_ref, 1, d)).astype(BF16)
    mif_ref[...] = _dot(h, wif_ref[...])


def _proj(x2, mod, g, wm, wg, wif, lbp, cw, cb, wq, wk, seq, tm):
    t, d = x2.shape
    per_b = seq // tm
    row = lambda w: pl.BlockSpec((tm, w), lambda i: (i, 0))
    sds = lambda w, dt: jax.ShapeDtypeStruct((t, w), dt)
    outs = [(HG_W, BF16)] * 4 + [(HG_W, F32)] + [(ML_W, BF16)] * 4 + [(d, BF16)] * 2 + [(LANES, F32)]
    consts = [g, wm, wg, wif, lbp, cw, cb, wq, wk]
    return pl.pallas_call(
        functools.partial(_proj_kernel, per_b=per_b),
        grid=(t // tm,),
        in_specs=[row(d), pl.BlockSpec((None, 1, mod.shape[-1]), lambda i: (i // per_b, 0, 0))]
                 + [_const_spec(a.shape) for a in consts],
        out_specs=[row(w) for w, _ in outs],
        out_shape=[sds(w, dt) for w, dt in outs],
        scratch_shapes=[pltpu.VMEM((8, ML_W), F32)],
        compiler_params=_cparams(("arbitrary",)),
        name="in_proj",
    )(x2, mod, g, *consts[1:])


def _head_norm(o, g):
    return o * lax.rsqrt(jnp.mean(o * o, axis=-1, keepdims=True) + EPS) * g


def _hgrn2_kernel(q_ref, k_ref, v_ref, lf_ref, gs_ref, g_ref, tri_ref, sel_ref, y_ref, st_ref):
    ts = q_ref.shape[0]
    nsub = HG_CHUNK // HG_SUB

    @pl.when(pl.program_id(1) == 0)
    def _():
        st_ref[...] = jnp.zeros_like(st_ref)

    tri = tri_ref[...]
    row = lax.broadcasted_iota(jnp.int32, (HG_CHUNK, HG_CHUNK), 0)
    col = lax.broadcasted_iota(jnp.int32, (HG_CHUNK, HG_CHUNK), 1)
    diag_mask = (row // HG_SUB == col // HG_SUB) & (col <= row)

    def chunk(c, carry):
        r0 = pl.multiple_of(c * HG_CHUNK, HG_CHUNK)
        rows = pl.ds(r0, HG_CHUNK)
        lf = lf_ref[rows, :]
        hi, lo = _split_bf16(lf)
        a_all = (_dot(tri, hi) + _dot(tri, lo)) * LOG2E
        p_heads = []
        per_head = []
        for hh in range(HG_HEADS):
            cols = slice(hh * HG_D, (hh + 1) * HG_D)
            a = a_all[:, cols]
            qf = q_ref[rows, cols].astype(F32)
            kf = k_ref[rows, cols].astype(F32)
            a4 = a.reshape(nsub, HG_SUB, HG_D)
            q4 = qf.reshape(nsub, HG_SUB, HG_D)
            k4 = kf.reshape(nsub, HG_SUB, HG_D)
            pieces = []
            for s in range(HG_SUB):
                dec = jnp.exp2(jnp.minimum(a4 - a4[:, s:s + 1, :], 0.0))
                pieces.append((q4 * k4[:, s:s + 1, :] * dec).reshape(HG_CHUNK, HG_D).astype(BF16))
            p_heads.append(jnp.concatenate(pieces, axis=1))
            per_head.append((a, qf, kf, cols))
        diag = _dot(jnp.concatenate(p_heads, axis=0), sel_ref[...])
        for hh, (a, qf, kf, cols) in enumerate(per_head):
            vb = v_ref[rows, cols]
            a_last = a[HG_CHUNK - 1:HG_CHUNK, :]
            qs, ks = [], []
            for j in range(nsub - 1):
                e = a[(j + 1) * HG_SUB - 1:(j + 1) * HG_SUB, :]
                lo_r, hi_r = j * HG_SUB, (j + 1) * HG_SUB
                kt = kf[lo_r:hi_r] * jnp.exp2(e - a[lo_r:hi_r])
                qt = qf[hi_r:] * jnp.exp2(a[hi_r:] - e)
                ks.append(jnp.concatenate(
                    [jnp.zeros((lo_r, HG_D), F32), kt, jnp.zeros((HG_CHUNK - hi_r, HG_D), F32)], axis=0)
                    if lo_r else jnp.concatenate([kt, jnp.zeros((HG_CHUNK - hi_r, HG_D), F32)], axis=0))
                qs.append(jnp.concatenate([jnp.zeros((hi_r, HG_D), F32), qt], axis=0))
            s_off = _dot_nt(jnp.concatenate(qs, axis=1).astype(BF16), jnp.concatenate(ks, axis=1).astype(BF16))
            s_diag = diag[hh * HG_CHUNK:(hh + 1) * HG_CHUNK, :HG_CHUNK]
            s_full = s_off + jnp.where(diag_mask, s_diag, 0.0)
            st = st_ref[hh]
            o = _dot(s_full.astype(BF16), vb) + _dot_nt((qf * jnp.exp2(a)).astype(BF16), st.astype(BF16))
            k_dec = (kf * jnp.exp2(a_last - a)).astype(BF16)
            st_ref[hh] = st * jnp.exp2(a_last) + _dot_tn(vb, k_dec)
            y = _head_norm(o, g_ref[:, cols]) * gs_ref[rows, cols].astype(F32)
            y_ref[rows, cols] = y.astype(BF16)
        return carry

    lax.fori_loop(0, ts // HG_CHUNK, chunk, 0, unroll=2)


def _hgrn2(q, k, v, lf, gs, g, batch, seq, ts):
    t = q.shape[0]
    per_b = seq // ts
    tri = jnp.tril(jnp.ones((HG_CHUNK, HG_CHUNK), F32)).astype(BF16)
    r = jnp.arange(HG_SUB * HG_D)[:, None] // HG_D
    cidx = jnp.arange(LANES)[None, :]
    sel = ((cidx % HG_SUB == r) & (cidx < HG_CHUNK)).astype(BF16)
    row = pl.BlockSpec((ts, HG_W), lambda b, s: (b * per_b + s, 0))
    return pl.pallas_call(
        _hgrn2_kernel,
        grid=(batch, per_b),
        in_specs=[row, row, row, row, row, _const_spec(g.shape), _const_spec(tri.shape), _const_spec(sel.shape)],
        out_specs=row,
        out_shape=jax.ShapeDtypeStruct((t, HG_W), BF16),
        scratch_shapes=[pltpu.VMEM((HG_HEADS, HG_D, HG_D), F32)],
        compiler_params=_cparams(("arbitrary", "arbitrary")),
        name="hgrn2",
    )(q, k, v, lf, gs, g, tri, sel)


def _mlstm_kernel(mq_ref, mk_ref, mv_ref, mo_ref, mif_ref, fb_ref, g_ref, tri_ref, y_ref, c_ref, m_ref):
    ts = mq_ref.shape[0]
    nh = ML_HEADS

    @pl.when(pl.program_id(1) == 0)
    def _():
        c_ref[...] = jnp.zeros_like(c_ref)
        m_ref[...] = jnp.zeros_like(m_ref)

    mif = mif_ref[...]
    lf = _log_sigmoid(mif + fb_ref[...])
    hi, lo = _split_bf16(lf)
    tri = tri_ref[...]
    bcum = _dot(tri, hi) + _dot(tri, lo)
    lane = lax.broadcasted_iota(jnp.int32, mif.shape, 1)
    z = jnp.where(lane < nh, mif, bcum)
    zt = z.T
    r_rows = zt[0:nh, :] - zt[nh:2 * nh, :]
    i_up = pltpu.roll(z, nh, axis=1)
    m_prev = m_ref[...]
    b_last = z[ts - 1:ts, :]

    row = lax.broadcasted_iota(jnp.int32, (ts, ts), 0)
    col = lax.broadcasted_iota(jnp.int32, (ts, ts), 1)
    causal = col <= row
    ones_col = (lax.broadcasted_iota(jnp.int32, (ts, ML_DV), 1) == 0).astype(BF16)

    d_logs = []
    m_intra = jnp.full(z.shape, -jnp.inf, F32)
    for hh in range(nh):
        d_log = jnp.where(causal, z[:, nh + hh:nh + hh + 1] + r_rows[hh:hh + 1, :], -jnp.inf)
        d_logs.append(d_log)
        m_intra = jnp.where(lane == nh + hh, jnp.max(d_log, axis=1, keepdims=True), m_intra)
    inter_log = z + m_prev
    m_t = jnp.maximum(m_intra, inter_log)
    w_inter = jnp.exp(inter_log - m_t)
    floor = jnp.exp(-m_t)
    state_log = b_last - z + i_up
    m_new = jnp.maximum(b_last + m_prev, jnp.max(state_log, axis=0, keepdims=True))
    w_s = jnp.exp(state_log - m_new)
    decay_prev = jnp.exp(b_last + m_prev - m_new)
    m_ref[...] = m_new

    for hh in range(nh):
        cols = slice(hh * ML_DV, (hh + 1) * ML_DV)
        sc = slice(nh + hh, nh + hh + 1)
        q = mq_ref[:, cols]
        k = mk_ref[:, cols]
        v_aug = jnp.concatenate([mv_ref[:, cols], ones_col], axis=1)
        c_prev = c_ref[hh]
        w = jnp.exp(d_logs[hh] - m_t[:, sc])
        p = (_dot_nt(q, k) * w).astype(BF16)
        tot = _dot(p, v_aug) + w_inter[:, sc] * _dot(q, c_prev.astype(BF16))
        den = tot[:, ML_DV:ML_DV + 1]
        hout = tot[:, :ML_DV] / jnp.maximum(jnp.abs(den), floor[:, sc])
        c_ref[hh] = decay_prev[:, sc] * c_prev + _dot_tn((w_s[:, sc] * k.astype(F32)).astype(BF16), v_aug)
        y = mo_ref[:, cols].astype(F32) * _head_norm(hout, g_ref[:, cols])
        y_ref[:, cols] = y.astype(BF16)


def _mlstm(mq, mk, mv, mo, mif, fb, g, batch, seq, ts):
    t = mq.shape[0]
    per_b = seq // ts
    tri = jnp.tril(jnp.ones((ts, ts), F32)).astype(BF16)
    row = lambda w: pl.BlockSpec((ts, w), lambda b, s: (b * per_b + s, 0))
    consts = [fb, g, tri]
    return pl.pallas_call(
        _mlstm_kernel,
        grid=(batch, per_b),
        in_specs=[row(ML_W)] * 4 + [row(LANES)] + [_const_spec(a.shape) for a in consts],
        out_specs=row(ML_W),
        out_shape=jax.ShapeDtypeStruct((t, ML_W), BF16),
        scratch_shapes=[pltpu.VMEM((ML_HEADS, ML_DV, 2 * ML_DV), F32), pltpu.VMEM((1, LANES), F32)],
        compiler_params=_cparams(("arbitrary", "arbitrary")),
        name="mlstm",
    )(mq, mk, mv, mo, mif, *consts)


def _store_token_tiles(ref, base, val):
    n, d = val.shape
    nj = d // LANES
    for j in range(nj):
        ref[pl.ds(base + j, n, stride=nj), :] = val[:, j * LANES:(j + 1) * LANES]


def _load_token_tiles(ref, base, n, d):
    nj = d // LANES
    return [ref[pl.ds(base + j, n, stride=nj), :] for j in range(nj)]


def _merge_kernel(yh_ref, ym_ref, ga_ref, gb_ref, x_ref, modm_ref, modf_ref, g_ref, wbh_ref, wbm_ref, wo_ref,
                  *rest, routed):
    d = x_ref.shape[-1]
    merged = (ga_ref[...].astype(F32) * _dot(yh_ref[...], wbh_ref[...])
              + gb_ref[...].astype(F32) * _dot(ym_ref[...], wbm_ref[...]))
    mix = _dot(merged.astype(BF16), wo_ref[...])
    x_new = x_ref[...] + modm_ref[:, 2 * d:] * mix
    modf = modf_ref[...]
    h = _norm_modulate(x_new, g_ref[...], modf[:, :d], modf[:, d:2 * d])
    if not routed:
        xo_ref, h_ref = rest
        xo_ref[...] = x_new
        h_ref[...] = h.astype(BF16)
        return
    rw_ref, rb_ref, xo_ref, h_ref, ridx_ref, gts_ref = rest
    xo_ref[...] = x_new
    _store_token_tiles(h_ref, 0, h)
    logits = _dot(h.astype(BF16), rw_ref[...]) + rb_ref[...]
    lane = lax.broadcasted_iota(jnp.int32, logits.shape, 1)
    logits = jnp.where(lane < N_EXPERTS, logits, -jnp.inf)
    m1 = jnp.max(logits, axis=1, keepdims=True)
    i1 = jnp.min(jnp.where(logits == m1, lane, LANES), axis=1, keepdims=True)
    others = jnp.where(lane == i1, -jnp.inf, logits)
    m2 = jnp.max(others, axis=1, keepdims=True)
    i2 = jnp.min(jnp.where(others == m2, lane, LANES), axis=1, keepdims=True)
    e = jnp.exp(m2 - m1)
    g1 = 1.0 / (1.0 + e)
    ridx_ref[...] = jnp.where(lane == 0, i1, jnp.where(lane == 1, i2, 0))
    gts_ref[...] = jnp.where(lane == 0, g1, jnp.where(lane == 1, e * g1, 0.0))


def _merge(yh, ym, ga, gb, x2, modm, modf, g, wbh, wbm, wo, router, seq, tm):
    t, d = x2.shape
    per_b = seq // tm
    routed = router is not None
    row = lambda w: pl.BlockSpec((tm, w), lambda i: (i, 0))
    mod_spec = pl.BlockSpec((None, 1, modm.shape[-1]), lambda i: (i // per_b, 0, 0))
    consts = [g, wbh, wbm, wo] + (list(router) if routed else [])
    nj = d // LANES
    if routed:
        outs = [(t, tm, d, F32), (t * nj, tm * nj, LANES, F32), (t, tm, LANES, jnp.int32), (t, tm, LANES, F32)]
    else:
        outs = [(t, tm, d, F32), (t, tm, d, BF16)]
    return pl.pallas_call(
        functools.partial(_merge_kernel, routed=routed),
        grid=(t // tm,),
        in_specs=[row(HG_W), row(ML_W), row(d), row(d), row(d), mod_spec, mod_spec]
                 + [_const_spec(a.shape) for a in consts],
        out_specs=[pl.BlockSpec((br, w), lambda i: (i, 0)) for _, br, w, _ in outs],
        out_shape=[jax.ShapeDtypeStruct((n, w), dt) for n, _, w, dt in outs],
        compiler_params=_cparams(("arbitrary",)),
        name="merge_out_proj_routed" if routed else "merge_out_proj",
    )(yh, ym, ga, gb, x2, modm, modf, *consts)


def _final_norm(out, fg_ref):
    return out * lax.rsqrt(jnp.mean(out * out, axis=-1, keepdims=True) + EPS) * fg_ref[...]


def _ffn_kernel(x_ref, h_ref, mod_ref, w1_ref, w3_ref, w2_ref, fg_ref, o_ref, acc_ref, *, final_norm):
    f = pl.program_id(1)
    d = x_ref.shape[-1]

    @pl.when(f == 0)
    def _():
        acc_ref[...] = jnp.zeros_like(acc_ref)

    h = h_ref[...]
    act = (_silu(_dot(h, w1_ref[...])) * _dot(h, w3_ref[...])).astype(BF16)
    acc_ref[...] += _dot(act, w2_ref[...])

    @pl.when(f == pl.num_programs(1) - 1)
    def _():
        out = x_ref[...] + mod_ref[:, 2 * d:] * acc_ref[...]
        o_ref[...] = _final_norm(out, fg_ref) if final_norm else out


def _ffn(x2, h, mod, w1, w3, w2, fg, seq, tm, tf, final_norm):
    t, d = x2.shape
    dff = w1.shape[-1]
    per_b = seq // tm
    row = lambda w: pl.BlockSpec((tm, w), lambda i, f: (i, 0))
    return pl.pallas_call(
        functools.partial(_ffn_kernel, final_norm=final_norm),
        grid=(t // tm, dff // tf),
        in_specs=[row(d), row(d),
                  pl.BlockSpec((None, 1, mod.shape[-1]), lambda i, f: (i // per_b, 0, 0)),
                  pl.BlockSpec((d, tf), lambda i, f: (0, f)),
                  pl.BlockSpec((d, tf), lambda i, f: (0, f)),
                  pl.BlockSpec((tf, d), lambda i, f: (f, 0)),
                  pl.BlockSpec(fg.shape, lambda i, f: (0, 0))],
        out_specs=row(d),
        out_shape=jax.ShapeDtypeStruct((t, d), F32),
        scratch_shapes=[pltpu.VMEM((tm, d), F32)],
        compiler_params=_cparams(("arbitrary", "arbitrary")),
        name="swiglu_dense",
    )(x2, h, mod, w1, w3, w2, fg)


def _route_plan(ridx, t, tm):
    n_assign = 2 * t
    n_tiles = n_assign // tm + N_EXPERTS
    e_flat = jnp.concatenate([ridx[:, 0], ridx[:, 1]])
    order = jnp.sort(e_flat * n_assign + jnp.arange(n_assign, dtype=jnp.int32)) % n_assign
    counts = jnp.sum((e_flat[:, None] == jnp.arange(N_EXPERTS, dtype=jnp.int32)[None, :]).astype(jnp.int32), axis=0)
    tiles_e = (counts + tm - 1) // tm
    tile_end = jnp.cumsum(tiles_e)
    n_used = tile_end[-1]
    seg_start = jnp.cumsum(counts) - counts
    ti = jnp.arange(n_tiles, dtype=jnp.int32)
    te = jnp.minimum(jnp.sum((ti[:, None] >= tile_end[None, :]).astype(jnp.int32), axis=1), N_EXPERTS - 1)
    j = ti - (tile_end - tiles_e)[te]
    n_valid = jnp.where(ti < n_used, jnp.clip(counts[te] - j * tm, 0, tm), 0)
    r = jnp.arange(tm, dtype=jnp.int32)[None, :]
    valid = r < n_valid[:, None]
    a = order[jnp.clip(seg_start[te][:, None] + j[:, None] * tm + r, 0, n_assign - 1)]
    gather_tok = jnp.where(valid, a % t, 0)
    scatter_row = jnp.where(valid, a, n_assign + (ti % 2)[:, None] * tm + r)
    scatter_row = jnp.concatenate([n_assign + tm + r, scatter_row], axis=0)
    tile_expert = jnp.where(ti < n_used, te, te[jnp.maximum(n_used - 1, 0)])
    return (gather_tok.reshape(n_tiles, 1, tm), scatter_row.reshape(n_tiles + 1, 1, tm),
            tile_expert.astype(jnp.int32), n_used.reshape(1).astype(jnp.int32))


def _moe_kernel(te_ref, nu_ref, gcur_ref, gnext_ref, sprev_ref, scur_ref, h_hbm, w1_ref, w3_ref, w2_ref, y_hbm,
                xbuf, ybuf, gsem, ssem):
    i = pl.program_id(0)
    n_used = nu_ref[0]
    d = w1_ref.shape[0]
    nj = d // LANES
    tm = xbuf.shape[0] // (2 * nj)
    slot = i % 2
    other = 1 - slot

    def rows(ref, r, n=1):
        return ref.at[pl.ds(pl.multiple_of(r * nj, nj), n * nj)]

    def gather_row(idx_ref, sl, r):
        return pltpu.make_async_copy(rows(h_hbm, idx_ref[0, r]), rows(xbuf, sl * tm + r), gsem.at[sl])

    def scatter_row(idx_ref, sl, r):
        return pltpu.make_async_copy(rows(ybuf, sl * tm + r), rows(y_hbm, idx_ref[0, r]), ssem.at[sl])

    def wait_slot(buf, sem, sl):
        whole = rows(buf, sl * tm, tm)
        pltpu.make_async_copy(whole, whole, sem.at[sl]).wait()

    def issue_rows(*copies):
        def body(rb, carry):
            for k in range(ISSUE_UNROLL):
                for make in copies:
                    make(rb * ISSUE_UNROLL + k).start()
            return carry
        lax.fori_loop(0, tm // ISSUE_UNROLL, body, 0)

    @pl.when(i == 0)
    def _():
        issue_rows(lambda r: gather_row(gcur_ref, 0, r))
        ybuf[pl.ds(tm * nj, tm * nj), :] = jnp.zeros((tm * nj, LANES), F32)
        for half in range(2):
            fill = pltpu.make_async_copy(rows(ybuf, tm, tm), rows(y_hbm, y_hbm.shape[0] // nj - (2 - half) * tm, tm),
                                         ssem.at[1])
            fill.start()
            fill.wait()

    @pl.when(i < n_used)
    def _():
        @pl.when(i >= 1)
        def _():
            wait_slot(ybuf, ssem, slot)

        wait_slot(xbuf, gsem, slot)
        issue_rows(lambda r: gather_row(gnext_ref, other, r), lambda r: scatter_row(sprev_ref, other, r))

        base = pl.multiple_of(slot * (tm * nj), tm * nj)
        x = jnp.concatenate(_load_token_tiles(xbuf, base, tm, d), axis=1).astype(BF16)
        act = (_silu(_dot(x, w1_ref[...])) * _dot(x, w3_ref[...])).astype(BF16)
        _store_token_tiles(ybuf, base, _dot(act, w2_ref[...]))

        @pl.when(i == n_used - 1)
        def _():
            issue_rows(lambda r: scatter_row(scur_ref, slot, r))
            wait_slot(ybuf, ssem, other)
            wait_slot(ybuf, ssem, slot)
            wait_slot(xbuf, gsem, other)


def _moe(h_tiles, plan, w1, w3, w2, tm):
    gather_tok, scatter_row, tile_expert, n_used = plan
    n_tiles = gather_tok.shape[0]
    _, d, dff = w1.shape
    nj = d // LANES
    t = h_tiles.shape[0] // nj
    idx_spec = lambda off, last: pl.BlockSpec(
        (None, 1, tm), lambda i, te, nu: (jnp.minimum(i + off, last), 0, 0), memory_space=pltpu.SMEM)
    buf = pltpu.VMEM((2 * tm * nj, LANES), F32)
    grid_spec = pltpu.PrefetchScalarGridSpec(
        num_scalar_prefetch=2,
        grid=(n_tiles,),
        in_specs=[idx_spec(0, n_tiles - 1), idx_spec(1, n_tiles - 1), idx_spec(0, n_tiles), idx_spec(1, n_tiles),
                  pl.BlockSpec(memory_space=pl.ANY),
                  pl.BlockSpec((None, d, dff), lambda i, te, nu: (te[i], 0, 0)),
                  pl.BlockSpec((None, d, dff), lambda i, te, nu: (te[i], 0, 0)),
                  pl.BlockSpec((None, dff, d), lambda i, te, nu: (te[i], 0, 0))],
        out_specs=pl.BlockSpec(memory_space=pl.ANY),
        scratch_shapes=[buf, buf, pltpu.SemaphoreType.DMA((2,)), pltpu.SemaphoreType.DMA((2,))],
    )
    return pl.pallas_call(
        _moe_kernel,
        grid_spec=grid_spec,
        out_shape=jax.ShapeDtypeStruct(((2 * t + 2 * tm) * nj, LANES), F32),
        compiler_params=_cparams(("arbitrary",)),
        name="moe_experts",
    )(tile_expert, n_used, gather_tok, gather_tok, scatter_row, scatter_row, h_tiles, w1, w3, w2)


def _combine_kernel(x_ref, y0_ref, y1_ref, gts_ref, mod_ref, fg_ref, o_ref, *, final_norm):
    tm, d = x_ref.shape
    gts = gts_ref[...]
    ff = jnp.concatenate(
        [gts[:, 0:1] * a + gts[:, 1:2] * b
         for a, b in zip(_load_token_tiles(y0_ref, 0, tm, d), _load_token_tiles(y1_ref, 0, tm, d))], axis=1)
    out = x_ref[...] + mod_ref[:, 2 * d:] * ff
    o_ref[...] = _final_norm(out, fg_ref) if final_norm else out


def _combine(x2, y_tiles, gts, mod, fg, seq, tm, final_norm):
    t, d = x2.shape
    nj = d // LANES
    per_b = seq // tm
    nb = t // tm
    row = lambda w: pl.BlockSpec((tm, w), lambda i: (i, 0))
    return pl.pallas_call(
        functools.partial(_combine_kernel, final_norm=final_norm),
        grid=(nb,),
        in_specs=[row(d), pl.BlockSpec((tm * nj, LANES), lambda i: (i, 0)),
                  pl.BlockSpec((tm * nj, LANES), lambda i: (nb + i, 0)), row(LANES),
                  pl.BlockSpec((None, 1, mod.shape[-1]), lambda i: (i // per_b, 0, 0)),
                  pl.BlockSpec(fg.shape, lambda i: (0, 0))],
        out_specs=row(d),
        out_shape=jax.ShapeDtypeStruct((t, d), F32),
        compiler_params=_cparams(("arbitrary",)),
        name="moe_combine",
    )(x2, y_tiles, y_tiles, gts, mod, fg)


def _pick(n, pref):
    while n % pref:
        pref //= 2
    return pref


@jax.jit
def kernel(x, c, norm_mix_g, ada_mix_w, ada_mix_b, w_in, hg_lb_logits, hg_norm_g, ml_conv_w, ml_conv_b, ml_wq,
           ml_wk, ml_fbias, ml_norm_g, w_br_hg, w_br_ml, w_out, norm_ffn_g, ada_ffn_w, ada_ffn_b, ffn_w1,
           ffn_w3, ffn_w2, moe_router_w, moe_router_b, moe_w1, moe_w3, moe_w2, final_norm_g):
    batch, seq, d = x.shape
    depth = w_in.shape[0]
    t = batch * seq
    tm = _pick(seq, 512)
    ts_hg = _pick(seq, 512)
    ts_ml = _pick(seq, 256)

    mod_mix, mod_ffn = _ada_params(c, ada_mix_w, ada_mix_b, ada_ffn_w, ada_ffn_b)
    mod_mix = mod_mix.reshape(depth, batch, 1, 3 * d)
    mod_ffn = mod_ffn.reshape(depth, batch, 1, 3 * d)

    lb_all = jnp.cumsum(jax.nn.softmax(hg_lb_logits.astype(F32), axis=0), axis=0)
    lb_all = lb_all - lb_all[0:1]

    n_main = 4 * HG_W + 3 * ML_W
    x2 = x.reshape(t, d)
    for l in range(depth):
        wl = w_in[l]
        wm = wl[:, :n_main].astype(BF16)
        wif = jnp.pad(wl[:, n_main:n_main + 2 * ML_HEADS], ((0, 0), (0, LANES - 2 * ML_HEADS))).astype(BF16)
        wg = wl[:, n_main + 2 * ML_HEADS:].astype(BF16)
        lb = lb_all[l][None, :]
        lbp = jnp.concatenate([jnp.log(lb), jnp.log1p(-lb), 1.0 - lb, jnp.zeros((5, HG_W), F32)], axis=0)
        pad_qk = lambda w: jnp.pad(w, ((0, 0), (0, 0), (0, ML_DV - ML_DQK))).astype(BF16)
        q, k, v, gs, lf, mq, mk, mv, mo, ga, gb, mif = _proj(
            x2, mod_mix[l], norm_mix_g[l][None, :], wm, wg, wif, lbp, ml_conv_w[l], ml_conv_b[l][None, :],
            pad_qk(ml_wq[l] * (ML_DQK ** -0.5)), pad_qk(ml_wk[l]), seq, tm)

        yh = _hgrn2(q, k, v, lf, gs, hg_norm_g[l][None, :], batch, seq, ts_hg)

        fb = jnp.zeros((1, LANES), F32).at[0, ML_HEADS:2 * ML_HEADS].set(ml_fbias[l])
        ym = _mlstm(mq, mk, mv, mo, mif, fb, ml_norm_g[l][None, :], batch, seq, ts_ml)

        j = l // 2
        routed = l % 2 == 1
        router = None
        if routed:
            router = (jnp.pad(moe_router_w[j], ((0, 0), (0, LANES - N_EXPERTS))).astype(BF16),
                      jnp.pad(moe_router_b[j], (0, LANES - N_EXPERTS))[None, :])
        merged = _merge(yh, ym, ga, gb, x2, mod_mix[l], mod_ffn[l], norm_ffn_g[l][None, :],
                        w_br_hg[l].astype(BF16), w_br_ml[l].astype(BF16), w_out[l].astype(BF16),
                        router, seq, tm)

        final = l == depth - 1
        fg = final_norm_g[None, :]
        if routed:
            x2, h2, ridx, gts = merged
            y = _moe(h2, _route_plan(ridx, t, tm), moe_w1[j].astype(BF16), moe_w3[j].astype(BF16),
                     moe_w2[j].astype(BF16), tm)
            x2 = _combine(x2, y, gts, mod_ffn[l], fg, seq, tm, final)
        else:
            x2, h2 = merged
            x2 = _ffn(x2, h2, mod_ffn[l], ffn_w1[j].astype(BF16), ffn_w3[j].astype(BF16),
                      ffn_w2[j].astype(BF16), fg, seq, tm, FF_TILE, final)
    return x2.reshape(batch, seq, d)
```

```python
import functools

import jax
import jax.numpy as jnp
from jax import lax
from jax.experimental import pallas as pl
from jax.experimental.pallas import tpu as pltpu

F32 = jnp.float32
BF16 = jnp.bfloat16

EPS = 1e-6
LANES = 128
ISSUE_UNROLL = 8
MOE_COL_CHUNKS = 4
HG_HEADS = 4
HG_D = 128
HG_W = HG_HEADS * HG_D
ML_HEADS = 4
ML_DV = 128
ML_DQK = 64
ML_W = ML_HEADS * ML_DV
CONV_K = 4
N_EXPERTS = 8
HG_CHUNK = 64
HG_SUB = 8
HG_GROUP = 4
ML_CHUNK = 256
ML_GROUP = 2
LOG2E = 1.4426950408889634
FF_TILE = 1408
VMEM_LIMIT = 56 * 1024 * 1024


def _cparams(sem):
    return pltpu.CompilerParams(dimension_semantics=sem, vmem_limit_bytes=VMEM_LIMIT)


def _const_spec(shape):
    nd = len(shape)
    return pl.BlockSpec(shape, lambda *_: (0,) * nd, pipeline_mode=pl.Buffered(1))


def _silu(v):
    return v * jax.nn.sigmoid(v)


def _log_sigmoid(z):
    return jnp.minimum(z, 0.0) - jnp.log1p(jnp.exp(-jnp.abs(z)))


def _split_bf16(v):
    hi = v.astype(BF16)
    lo = (v - hi.astype(F32)).astype(BF16)
    return hi, lo


def _dot(a, b):
    return jnp.dot(a, b, preferred_element_type=F32)


def _dot_nt(a, b):
    return lax.dot_general(a, b, (((1,), (1,)), ((), ())), preferred_element_type=F32)


def _dot_tn(a, b):
    return lax.dot_general(a, b, (((0,), (0,)), ((), ())), preferred_element_type=F32)


def _ada_kernel(c_ref, wm_ref, bm_ref, wf_ref, bf_ref, om_ref, of_ref):
    c = c_ref[...]
    ca = _silu(c).astype(BF16)
    om_ref[...] = _dot(ca, wm_ref[...].astype(BF16)) + bm_ref[...]
    of_ref[...] = _dot(ca, wf_ref[...].astype(BF16)) + bf_ref[...]


def _ada_params(c, wm, bm, wf, bf):
    depth, d, d3 = wm.shape
    b = c.shape[0]
    nt = d3 // d
    w_spec = pl.BlockSpec((None, d, d), lambda l, j: (l, 0, j))
    b_spec = pl.BlockSpec((None, 1, d), lambda l, j: (l, 0, j))
    o_spec = pl.BlockSpec((None, b, d), lambda l, j: (l, 0, j))
    out = jax.ShapeDtypeStruct((depth, b, d3), F32)
    return pl.pallas_call(
        _ada_kernel,
        grid=(depth, nt),
        in_specs=[pl.BlockSpec((b, d), lambda l, j: (0, 0)), w_spec, b_spec, w_spec, b_spec],
        out_specs=[o_spec, o_spec],
        out_shape=[out, out],
        compiler_params=_cparams(("arbitrary", "arbitrary")),
        name="ada_params",
    )(c, wm, bm.reshape(depth, 1, d3), wf, bf.reshape(depth, 1, d3))


def _norm_modulate(x, g, shift, scale):
    ms = jnp.mean(x * x, axis=-1, keepdims=True)
    y = x * lax.rsqrt(ms + EPS) * g
    return y * (1.0 + scale) + shift


def _proj_kernel(x_ref, mod_ref, g_ref, wm_ref, wg_ref, wif_ref, lbp_ref, cw_ref, cb_ref, wq_ref, wk_ref,
                 q_ref, k_ref, v_ref, gs_ref, lf_ref, mq_ref, mk_ref, mv_ref, mo_ref, ga_ref, gb_ref, mif_ref,
                 carry_ref, *, per_b):
    tm, d = x_ref.shape
    mod = mod_ref[...]
    h = _norm_modulate(x_ref[...], g_ref[...], mod[:, :d], mod[:, d:2 * d]).astype(BF16)

    def piece(w_ref, j, width=HG_W):
        return _dot(h, w_ref[:, j * width:(j + 1) * width])

    q_ref[...] = _silu(piece(wm_ref, 0)).astype(BF16)
    zf = piece(wm_ref, 1)
    log_lb, log1m_lb, one_m_lb = lbp_ref[0:1, :], lbp_ref[1:2, :], lbp_ref[2:3, :]
    t = log1m_lb + _log_sigmoid(zf)
    lf_ref[...] = jnp.maximum(log_lb, t) + jnp.log1p(jnp.exp(-jnp.abs(log_lb - t)))
    k_ref[...] = (one_m_lb * jax.nn.sigmoid(-zf)).astype(BF16)
    v_ref[...] = piece(wm_ref, 2).astype(BF16)
    gs_ref[...] = _silu(piece(wm_ref, 3)).astype(BF16)
    @pl.when(pl.program_id(0) % per_b == 0)
    def _():
        carry_ref[...] = jnp.zeros_like(carry_ref)

    u = piece(wm_ref, 4)
    ext = jnp.concatenate([carry_ref[...], u], axis=0)
    carry_ref[...] = u[tm - 8:, :]
    conv = cb_ref[...]
    for j in range(CONV_K):
        off = 8 - (CONV_K - 1) + j
        conv = conv + cw_ref[j:j + 1, :] * ext[off:off + tm, :]
    uc = _silu(conv).astype(BF16)
    for hh in range(ML_HEADS):
        cols = slice(hh * ML_DV, (hh + 1) * ML_DV)
        mq_ref[:, cols] = _dot(uc[:, cols], wq_ref[hh]).astype(BF16)
        mk_ref[:, cols] = _dot(uc[:, cols], wk_ref[hh]).astype(BF16)
    mv_ref[...] = piece(wm_ref, 5).astype(BF16)
    mo_ref[...] = jax.nn.sigmoid(piece(wm_ref, 6)).astype(BF16)
    ga_ref[...] = jax.nn.sigmoid(piece(wg_ref, 0, d)).astype(BF16)
    gb_ref[...] = jax.nn.sigmoid(piece(wg_ref, 1, d)).astype(BF16)
    mif_ref[...] = _dot(h, wif_ref[...])


def _proj(x2, mod, g, wm, wg, wif, lbp, cw, cb, wq, wk, seq, tm):
    t, d = x2.shape
    per_b = seq // tm
    row = lambda w: pl.BlockSpec((tm, w), lambda i: (i, 0))
    sds = lambda w, dt: jax.ShapeDtypeStruct((t, w), dt)
    outs = [(HG_W, BF16)] * 4 + [(HG_W, F32)] + [(ML_W, BF16)] * 4 + [(d, BF16)] * 2 + [(LANES, F32)]
    consts = [g, wm, wg, wif, lbp, cw, cb, wq, wk]
    return pl.pallas_call(
        functools.partial(_proj_kernel, per_b=per_b),
        grid=(t // tm,),
        in_specs=[row(d), pl.BlockSpec((None, 1, mod.shape[-1]), lambda i: (i // per_b, 0, 0))]
                 + [_const_spec(a.shape) for a in consts],
        out_specs=[row(w) for w, _ in outs],
        out_shape=[sds(w, dt) for w, dt in outs],
        scratch_shapes=[pltpu.VMEM((8, ML_W), F32)],
        compiler_params=_cparams(("arbitrary",)),
        name="in_proj",
    )(x2, mod, g, *consts[1:])


def _head_norm(o, g):
    return o * lax.rsqrt(jnp.mean(o * o, axis=-1, keepdims=True) + EPS) * g


def _hgrn2_kernel(q_ref, k_ref, v_ref, lf_ref, gs_ref, g_ref, tri_ref, sel_ref, y_ref, st_ref):
    ts = q_ref.shape[0]
    nsub = HG_CHUNK // HG_SUB

    @pl.when(pl.program_id(1) == 0)
    def _():
        st_ref[...] = jnp.zeros_like(st_ref)

    tri = tri_ref[...]
    row = lax.broadcasted_iota(jnp.int32, (HG_CHUNK, HG_CHUNK), 0)
    col = lax.broadcasted_iota(jnp.int32, (HG_CHUNK, HG_CHUNK), 1)
    diag_mask = (row // HG_SUB == col // HG_SUB) & (col <= row)

    def intra(c):
        rows = pl.ds(pl.multiple_of(c * HG_CHUNK, HG_CHUNK), HG_CHUNK)
        hi, lo = _split_bf16(lf_ref[rows, :])
        a_all = (_dot(tri, hi) + _dot(tri, lo)) * LOG2E
        heads = []
        for hh in range(HG_HEADS):
            cols = slice(hh * HG_D, (hh + 1) * HG_D)
            heads.append((a_all[:, cols], q_ref[rows, cols].astype(F32), k_ref[rows, cols].astype(F32),
                          v_ref[rows, cols], gs_ref[rows, cols]))
        p_heads = []
        for a, qf, kf, _, _ in heads:
            a4 = a.reshape(nsub, HG_SUB, HG_D)
            q4 = qf.reshape(nsub, HG_SUB, HG_D)
            k4 = kf.reshape(nsub, HG_SUB, HG_D)
            pieces = []
            for s in range(HG_SUB):
                dec = jnp.exp2(jnp.minimum(a4 - a4[:, s:s + 1, :], 0.0))
                pieces.append((q4 * k4[:, s:s + 1, :] * dec).reshape(HG_CHUNK, HG_D).astype(BF16))
            p_heads.append(jnp.concatenate(pieces, axis=1))
        diag = _dot(jnp.concatenate(p_heads, axis=0), sel_ref[...])
        out = []
        for hh, (a, qf, kf, vb, gs) in enumerate(heads):
            a_last = a[HG_CHUNK - 1:HG_CHUNK, :]
            qs, ks = [], []
            for j in range(nsub - 1):
                e = a[(j + 1) * HG_SUB - 1:(j + 1) * HG_SUB, :]
                lo_r, hi_r = j * HG_SUB, (j + 1) * HG_SUB
                kt = kf[lo_r:hi_r] * jnp.exp2(e - a[lo_r:hi_r])
                qt = qf[hi_r:] * jnp.exp2(a[hi_r:] - e)
                ks.append(jnp.concatenate(
                    [jnp.zeros((lo_r, HG_D), F32), kt, jnp.zeros((HG_CHUNK - hi_r, HG_D), F32)], axis=0)
                    if lo_r else jnp.concatenate([kt, jnp.zeros((HG_CHUNK - hi_r, HG_D), F32)], axis=0))
                qs.append(jnp.concatenate([jnp.zeros((hi_r, HG_D), F32), qt], axis=0))
            s_off = _dot_nt(jnp.concatenate(qs, axis=1).astype(BF16), jnp.concatenate(ks, axis=1).astype(BF16))
            s_diag = diag[hh * HG_CHUNK:(hh + 1) * HG_CHUNK, :HG_CHUNK]
            s_full = s_off + jnp.where(diag_mask, s_diag, 0.0)
            k_dec = (kf * jnp.exp2(a_last - a)).astype(BF16)
            out.append((_dot(s_full.astype(BF16), vb), (qf * jnp.exp2(a)).astype(BF16), jnp.exp2(a_last),
                        _dot_tn(vb, k_dec), gs))
        return out

    def group(gi, states):
        parts = [intra(gi * HG_GROUP + c) for c in range(HG_GROUP)]
        states = list(states)
        ys = []
        for part in parts:
            y_heads = []
            for hh, (o_intra, q_dec, decay, incr, gs) in enumerate(part):
                st = states[hh]
                o = o_intra + _dot_nt(q_dec, st.astype(BF16))
                states[hh] = st * decay + incr
                cols = slice(hh * HG_D, (hh + 1) * HG_D)
                y_heads.append((_head_norm(o, g_ref[:, cols]) * gs.astype(F32)).astype(BF16))
            ys.append(jnp.concatenate(y_heads, axis=1))
        for c, y in enumerate(ys):
            y_ref[pl.ds(pl.multiple_of((gi * HG_GROUP + c) * HG_CHUNK, HG_CHUNK), HG_CHUNK), :] = y
        return tuple(states)

    states = lax.fori_loop(0, ts // (HG_CHUNK * HG_GROUP), group, tuple(st_ref[hh] for hh in range(HG_HEADS)))
    for hh in range(HG_HEADS):
        st_ref[hh] = states[hh]


def _hgrn2(q, k, v, lf, gs, g, batch, seq, ts):
    t = q.shape[0]
    per_b = seq // ts
    tri = jnp.tril(jnp.ones((HG_CHUNK, HG_CHUNK), F32)).astype(BF16)
    r = jnp.arange(HG_SUB * HG_D)[:, None] // HG_D
    cidx = jnp.arange(LANES)[None, :]
    sel = ((cidx % HG_SUB == r) & (cidx < HG_CHUNK)).astype(BF16)
    row = pl.BlockSpec((ts, HG_W), lambda b, s: (b * per_b + s, 0))
    return pl.pallas_call(
        _hgrn2_kernel,
        grid=(batch, per_b),
        in_specs=[row, row, row, row, row, _const_spec(g.shape), _const_spec(tri.shape), _const_spec(sel.shape)],
        out_specs=row,
        out_shape=jax.ShapeDtypeStruct((t, HG_W), BF16),
        scratch_shapes=[pltpu.VMEM((HG_HEADS, HG_D, HG_D), F32)],
        compiler_params=_cparams(("arbitrary", "arbitrary")),
        name="hgrn2",
    )(q, k, v, lf, gs, g, tri, sel)


def _mlstm_kernel(mq_ref, mk_ref, mv_ref, mo_ref, mif_ref, fb_ref, g_ref, tri_ref, y_ref, c_ref, m_ref):
    ts = mq_ref.shape[0]
    nh = ML_HEADS
    n_chunks = ts // ML_CHUNK

    @pl.when(pl.program_id(1) == 0)
    def _():
        c_ref[...] = jnp.zeros_like(c_ref)
        m_ref[...] = jnp.zeros_like(m_ref)

    tri = tri_ref[...]
    row = lax.broadcasted_iota(jnp.int32, (ML_CHUNK, ML_CHUNK), 0)
    col = lax.broadcasted_iota(jnp.int32, (ML_CHUNK, ML_CHUNK), 1)
    causal = col <= row
    ones_blk = jnp.ones((ML_CHUNK, ML_DV), BF16)
    lane = lax.broadcasted_iota(jnp.int32, (ML_CHUNK, LANES), 1)
    lane8 = lax.broadcasted_iota(jnp.int32, (2 * nh, ML_CHUNK), 1)

    m_prev = m_ref[...]
    gates = []
    for c in range(n_chunks):
        rows = slice(c * ML_CHUNK, (c + 1) * ML_CHUNK)
        mif = mif_ref[rows, :]
        hi, lo = _split_bf16(_log_sigmoid(mif + fb_ref[...]))
        bcum = _dot(tri, hi) + _dot(tri, lo)
        z = jnp.where(lane < nh, mif, bcum)
        b_last = z[ML_CHUNK - 1:ML_CHUNK, :]
        state_log = b_last - z + pltpu.roll(z, nh, axis=1)
        m_new = jnp.maximum(b_last + m_prev, jnp.max(state_log, axis=0, keepdims=True))
        gates.append((z, z + m_prev, jnp.exp(state_log - m_new), jnp.exp(b_last + m_prev - m_new)))
        m_prev = m_new
    m_ref[...] = m_prev

    parts = []
    for c, (z, inter_log, w_s, decay_prev) in enumerate(gates):
        rows = slice(c * ML_CHUNK, (c + 1) * ML_CHUNK)
        zt8 = z.T[0:2 * nh, :]
        r8 = pltpu.roll(zt8, nh, axis=0) - zt8
        cm = r8
        shift = 1
        while shift < ML_CHUNK:
            cm = jnp.maximum(cm, jnp.where(lane8 >= shift, pltpu.roll(cm, shift, axis=1), -jnp.inf))
            shift *= 2
        cm_cols = jnp.concatenate([cm, jnp.zeros((LANES - 2 * nh, ML_CHUNK), F32)], axis=0).T
        m_t = jnp.maximum(z + cm_cols, inter_log)
        w_inter = jnp.exp(inter_log - m_t)
        floor = jnp.exp(-m_t)
        g2 = (z - m_t) * LOG2E
        r2 = r8 * LOG2E
        heads = []
        for hh in range(nh):
            cols = slice(hh * ML_DV, (hh + 1) * ML_DV)
            sc = slice(nh + hh, nh + hh + 1)
            q = mq_ref[rows, cols]
            k = mk_ref[rows, cols]
            v_aug = jnp.concatenate([mv_ref[rows, cols], ones_blk], axis=1)
            w = jnp.where(causal, jnp.exp2(g2[:, sc] + r2[nh + hh:nh + hh + 1, :]), 0.0)
            p = (_dot_nt(q, k) * w).astype(BF16)
            heads.append((q, _dot(p, v_aug), w_inter[:, sc], floor[:, sc], decay_prev[:, sc],
                          _dot_tn((w_s[:, sc] * k.astype(F32)).astype(BF16), v_aug), mo_ref[rows, cols]))
        parts.append(heads)

    c_state = [c_ref[hh] for hh in range(nh)]
    ys = []
    for heads in parts:
        y_heads = []
        for hh, (q, pv, w_inter, floor, decay_prev, incr, mo) in enumerate(heads):
            cols = slice(hh * ML_DV, (hh + 1) * ML_DV)
            tot = pv + w_inter * _dot(q, c_state[hh].astype(BF16))
            c_state[hh] = decay_prev * c_state[hh] + incr
            hout = tot[:, :ML_DV] / jnp.maximum(jnp.abs(tot[:, ML_DV:]), floor)
            y_heads.append((mo.astype(F32) * _head_norm(hout, g_ref[:, cols])).astype(BF16))
        ys.append(jnp.concatenate(y_heads, axis=1))
    for c, y in enumerate(ys):
        y_ref[c * ML_CHUNK:(c + 1) * ML_CHUNK, :] = y
    for hh in range(nh):
        c_ref[hh] = c_state[hh]


def _mlstm(mq, mk, mv, mo, mif, fb, g, batch, seq, ts):
    t = mq.shape[0]
    per_b = seq // ts
    tri = jnp.tril(jnp.ones((ML_CHUNK, ML_CHUNK), F32)).astype(BF16)
    row = lambda w: pl.BlockSpec((ts, w), lambda b, s: (b * per_b + s, 0))
    consts = [fb, g, tri]
    return pl.pallas_call(
        _mlstm_kernel,
        grid=(batch, per_b),
        in_specs=[row(ML_W)] * 4 + [row(LANES)] + [_const_spec(a.shape) for a in consts],
        out_specs=row(ML_W),
        out_shape=jax.ShapeDtypeStruct((t, ML_W), BF16),
        scratch_shapes=[pltpu.VMEM((ML_HEADS, ML_DV, 2 * ML_DV), F32), pltpu.VMEM((1, LANES), F32)],
        compiler_params=_cparams(("arbitrary", "arbitrary")),
        name="mlstm",
    )(mq, mk, mv, mo, mif, *consts)


def _store_token_tiles(ref, base, val):
    n, d = val.shape
    nj = d // LANES
    for j in range(nj):
        ref[pl.ds(base + j, n, stride=nj), :] = val[:, j * LANES:(j + 1) * LANES]


def _load_token_tiles(ref, base, n, d):
    nj = d // LANES
    return [ref[pl.ds(base + j, n, stride=nj), :] for j in range(nj)]


def _merge_kernel(yh_ref, ym_ref, ga_ref, gb_ref, x_ref, modm_ref, modf_ref, g_ref, wbh_ref, wbm_ref, wo_ref,
                  *rest, routed):
    d = x_ref.shape[-1]
    merged = (ga_ref[...].astype(F32) * _dot(yh_ref[...], wbh_ref[...])
              + gb_ref[...].astype(F32) * _dot(ym_ref[...], wbm_ref[...]))
    mix = _dot(merged.astype(BF16), wo_ref[...])
    x_new = x_ref[...] + modm_ref[:, 2 * d:] * mix
    modf = modf_ref[...]
    h = _norm_modulate(x_new, g_ref[...], modf[:, :d], modf[:, d:2 * d])
    if not routed:
        xo_ref, h_ref = rest
        xo_ref[...] = x_new
        h_ref[...] = h.astype(BF16)
        return
    rw_ref, rb_ref, xo_ref, h_ref, ridx_ref, gts_ref = rest
    xo_ref[...] = x_new
    _store_token_tiles(h_ref, 0, h)
    logits = _dot(h.astype(BF16), rw_ref[...]) + rb_ref[...]
    lane = lax.broadcasted_iota(jnp.int32, logits.shape, 1)
    logits = jnp.where(lane < N_EXPERTS, logits, -jnp.inf)
    m1 = jnp.max(logits, axis=1, keepdims=True)
    i1 = jnp.min(jnp.where(logits == m1, lane, LANES), axis=1, keepdims=True)
    others = jnp.where(lane == i1, -jnp.inf, logits)
    m2 = jnp.max(others, axis=1, keepdims=True)
    i2 = jnp.min(jnp.where(others == m2, lane, LANES), axis=1, keepdims=True)
    e = jnp.exp(m2 - m1)
    g1 = 1.0 / (1.0 + e)
    ridx_ref[...] = jnp.where(lane == 0, i1, jnp.where(lane == 1, i2, 0))
    gts_ref[...] = jnp.where(lane == 0, g1, jnp.where(lane == 1, e * g1, 0.0))


def _merge(yh, ym, ga, gb, x2, modm, modf, g, wbh, wbm, wo, router, seq, tm):
    t, d = x2.shape
    per_b = seq // tm
    routed = router is not None
    row = lambda w: pl.BlockSpec((tm, w), lambda i: (i, 0))
    mod_spec = pl.BlockSpec((None, 1, modm.shape[-1]), lambda i: (i // per_b, 0, 0))
    consts = [g, wbh, wbm, wo] + (list(router) if routed else [])
    nj = d // LANES
    if routed:
        outs = [(t, tm, d, F32), (t * nj, tm * nj, LANES, F32), (t, tm, LANES, jnp.int32), (t, tm, LANES, F32)]
    else:
        outs = [(t, tm, d, F32), (t, tm, d, BF16)]
    return pl.pallas_call(
        functools.partial(_merge_kernel, routed=routed),
        grid=(t // tm,),
        in_specs=[row(HG_W), row(ML_W), row(d), row(d), row(d), mod_spec, mod_spec]
                 + [_const_spec(a.shape) for a in consts],
        out_specs=[pl.BlockSpec((br, w), lambda i: (i, 0)) for _, br, w, _ in outs],
        out_shape=[jax.ShapeDtypeStruct((n, w), dt) for n, _, w, dt in outs],
        compiler_params=_cparams(("arbitrary",)),
        name="merge_out_proj_routed" if routed else "merge_out_proj",
    )(yh, ym, ga, gb, x2, modm, modf, *consts)


def _final_norm(out, fg_ref):
    return out * lax.rsqrt(jnp.mean(out * out, axis=-1, keepdims=True) + EPS) * fg_ref[...]


def _ffn_kernel(x_ref, h_ref, mod_ref, w1_ref, w3_ref, w2_ref, fg_ref, o_ref, acc_ref, *, final_norm):
    f = pl.program_id(1)
    d = x_ref.shape[-1]

    @pl.when(f == 0)
    def _():
        acc_ref[...] = jnp.zeros_like(acc_ref)

    h = h_ref[...]
    act = (_silu(_dot(h, w1_ref[...])) * _dot(h, w3_ref[...])).astype(BF16)
    acc_ref[...] += _dot(act, w2_ref[...])

    @pl.when(f == pl.num_programs(1) - 1)
    def _():
        out = x_ref[...] + mod_ref[:, 2 * d:] * acc_ref[...]
        o_ref[...] = _final_norm(out, fg_ref) if final_norm else out


def _ffn(x2, h, mod, w1, w3, w2, fg, seq, tm, tf, final_norm):
    t, d = x2.shape
    dff = w1.shape[-1]
    per_b = seq // tm
    row = lambda w: pl.BlockSpec((tm, w), lambda i, f: (i, 0))
    return pl.pallas_call(
        functools.partial(_ffn_kernel, final_norm=final_norm),
        grid=(t // tm, dff // tf),
        in_specs=[row(d), row(d),
                  pl.BlockSpec((None, 1, mod.shape[-1]), lambda i, f: (i // per_b, 0, 0)),
                  pl.BlockSpec((d, tf), lambda i, f: (0, f)),
                  pl.BlockSpec((d, tf), lambda i, f: (0, f)),
                  pl.BlockSpec((tf, d), lambda i, f: (f, 0)),
                  pl.BlockSpec(fg.shape, lambda i, f: (0, 0))],
        out_specs=row(d),
        out_shape=jax.ShapeDtypeStruct((t, d), F32),
        scratch_shapes=[pltpu.VMEM((tm, d), F32)],
        compiler_params=_cparams(("arbitrary", "arbitrary")),
        name="swiglu_dense",
    )(x2, h, mod, w1, w3, w2, fg)


def _route_plan(ridx, t, tm):
    n_assign = 2 * t
    n_tiles = n_assign // tm + N_EXPERTS
    e_flat = jnp.concatenate([ridx[:, 0], ridx[:, 1]])
    order = jnp.sort(e_flat * n_assign + jnp.arange(n_assign, dtype=jnp.int32)) % n_assign
    counts = jnp.sum((e_flat[:, None] == jnp.arange(N_EXPERTS, dtype=jnp.int32)[None, :]).astype(jnp.int32), axis=0)
    tiles_e = (counts + tm - 1) // tm
    tile_end = jnp.cumsum(tiles_e)
    n_used = tile_end[-1]
    seg_start = jnp.cumsum(counts) - counts
    ti = jnp.arange(n_tiles, dtype=jnp.int32)
    te = jnp.minimum(jnp.sum((ti[:, None] >= tile_end[None, :]).astype(jnp.int32), axis=1), N_EXPERTS - 1)
    j = ti - (tile_end - tiles_e)[te]
    n_valid = jnp.where(ti < n_used, jnp.clip(counts[te] - j * tm, 0, tm), 0)
    r = jnp.arange(tm, dtype=jnp.int32)[None, :]
    valid = r < n_valid[:, None]
    a = order[jnp.clip(seg_start[te][:, None] + j[:, None] * tm + r, 0, n_assign - 1)]
    gather_tok = jnp.where(valid, a % t, 0)
    scatter_row = jnp.where(valid, a, n_assign + (ti % 2)[:, None] * tm + r)
    scatter_row = jnp.concatenate([n_assign + tm + r, scatter_row], axis=0)
    tile_expert = jnp.where(ti < n_used, te, te[jnp.maximum(n_used - 1, 0)])
    return (gather_tok.reshape(n_tiles, 1, tm), scatter_row.reshape(n_tiles + 1, 1, tm),
            tile_expert.astype(jnp.int32), n_used.reshape(1).astype(jnp.int32))


def _moe_kernel(te_ref, nu_ref, gcur_ref, gnext_ref, sprev_ref, scur_ref, h_hbm, w1_ref, w3_ref, w2_ref, y_hbm,
                xbuf, ybuf, gsem, ssem):
    i = pl.program_id(0)
    n_used = nu_ref[0]
    d = w1_ref.shape[0]
    nj = d // LANES
    tm = xbuf.shape[0] // (2 * nj)
    slot = i % 2
    other = 1 - slot

    def rows(ref, r, n=1):
        return ref.at[pl.ds(pl.multiple_of(r * nj, nj), n * nj)]

    def gather_row(idx_ref, sl, r):
        return pltpu.make_async_copy(rows(h_hbm, idx_ref[0, r]), rows(xbuf, sl * tm + r), gsem.at[sl])

    def scatter_row(idx_ref, sl, r):
        return pltpu.make_async_copy(rows(ybuf, sl * tm + r), rows(y_hbm, idx_ref[0, r]), ssem.at[sl])

    def wait_slot(buf, sem, sl):
        whole = rows(buf, sl * tm, tm)
        pltpu.make_async_copy(whole, whole, sem.at[sl]).wait()

    def issue_rows(*copies):
        def body(rb, carry):
            for k in range(ISSUE_UNROLL):
                for make in copies:
                    make(rb * ISSUE_UNROLL + k).start()
            return carry
        lax.fori_loop(0, tm // ISSUE_UNROLL, body, 0)

    @pl.when(i == 0)
    def _():
        issue_rows(lambda r: gather_row(gcur_ref, 0, r))
        ybuf[pl.ds(tm * nj, tm * nj), :] = jnp.zeros((tm * nj, LANES), F32)
        for half in range(2):
            fill = pltpu.make_async_copy(rows(ybuf, tm, tm), rows(y_hbm, y_hbm.shape[0] // nj - (2 - half) * tm, tm),
                                         ssem.at[1])
            fill.start()
            fill.wait()

    @pl.when(i < n_used)
    def _():
        @pl.when(i >= 1)
        def _():
            wait_slot(ybuf, ssem, slot)

        wait_slot(xbuf, gsem, slot)
        base = pl.multiple_of(slot * (tm * nj), tm * nj)
        x = jnp.concatenate(_load_token_tiles(xbuf, base, tm, d), axis=1).astype(BF16)
        act = (_silu(_dot(x, w1_ref[...])) * _dot(x, w3_ref[...])).astype(BF16)

        rows_per = tm // MOE_COL_CHUNKS
        tiles_per = nj // MOE_COL_CHUNKS

        def col_chunk(cc, carry):
            for k in range(rows_per):
                gather_row(gnext_ref, other, cc * rows_per + k).start()
                scatter_row(sprev_ref, other, cc * rows_per + k).start()
            c0 = pl.multiple_of(cc * (tiles_per * LANES), tiles_per * LANES)
            yc = _dot(act, w2_ref[:, pl.ds(c0, tiles_per * LANES)])
            for jj in range(tiles_per):
                ybuf[pl.ds(base + cc * tiles_per + jj, tm, stride=nj), :] = yc[:, jj * LANES:(jj + 1) * LANES]
            return carry
        lax.fori_loop(0, MOE_COL_CHUNKS, col_chunk, 0)

        @pl.when(i == n_used - 1)
        def _():
            issue_rows(lambda r: scatter_row(scur_ref, slot, r))
            wait_slot(ybuf, ssem, other)
            wait_slot(ybuf, ssem, slot)
            wait_slot(xbuf, gsem, other)


def _moe(h_tiles, plan, w1, w3, w2, tm):
    gather_tok, scatter_row, tile_expert, n_used = plan
    n_tiles = gather_tok.shape[0]
    _, d, dff = w1.shape
    nj = d // LANES
    t = h_tiles.shape[0] // nj
    idx_spec = lambda off, last: pl.BlockSpec(
        (None, 1, tm), lambda i, te, nu: (jnp.minimum(i + off, last), 0, 0), memory_space=pltpu.SMEM)
    buf = pltpu.VMEM((2 * tm * nj, LANES), F32)
    grid_spec = pltpu.PrefetchScalarGridSpec(
        num_scalar_prefetch=2,
        grid=(n_tiles,),
        in_specs=[idx_spec(0, n_tiles - 1), idx_spec(1, n_tiles - 1), idx_spec(0, n_tiles), idx_spec(1, n_tiles),
                  pl.BlockSpec(memory_space=pl.ANY),
                  pl.BlockSpec((None, d, dff), lambda i, te, nu: (te[i], 0, 0)),
                  pl.BlockSpec((None, d, dff), lambda i, te, nu: (te[i], 0, 0)),
                  pl.BlockSpec((None, dff, d), lambda i, te, nu: (te[i], 0, 0))],
        out_specs=pl.BlockSpec(memory_space=pl.ANY),
        scratch_shapes=[buf, buf, pltpu.SemaphoreType.DMA((2,)), pltpu.SemaphoreType.DMA((2,))],
    )
    return pl.pallas_call(
        _moe_kernel,
        grid_spec=grid_spec,
        out_shape=jax.ShapeDtypeStruct(((2 * t + 2 * tm) * nj, LANES), F32),
        compiler_params=_cparams(("arbitrary",)),
        name="moe_experts",
    )(tile_expert, n_used, gather_tok, gather_tok, scatter_row, scatter_row, h_tiles, w1, w3, w2)


def _combine_kernel(x_ref, y0_ref, y1_ref, gts_ref, mod_ref, fg_ref, o_ref, *, final_norm):
    tm, d = x_ref.shape
    gts = gts_ref[...]
    ff = jnp.concatenate(
        [gts[:, 0:1] * a + gts[:, 1:2] * b
         for a, b in zip(_load_token_tiles(y0_ref, 0, tm, d), _load_token_tiles(y1_ref, 0, tm, d))], axis=1)
    out = x_ref[...] + mod_ref[:, 2 * d:] * ff
    o_ref[...] = _final_norm(out, fg_ref) if final_norm else out


def _combine(x2, y_tiles, gts, mod, fg, seq, tm, final_norm):
    t, d = x2.shape
    nj = d // LANES
    per_b = seq // tm
    nb = t // tm
    row = lambda w: pl.BlockSpec((tm, w), lambda i: (i, 0))
    return pl.pallas_call(
        functools.partial(_combine_kernel, final_norm=final_norm),
        grid=(nb,),
        in_specs=[row(d), pl.BlockSpec((tm * nj, LANES), lambda i: (i, 0)),
                  pl.BlockSpec((tm * nj, LANES), lambda i: (nb + i, 0)), row(LANES),
                  pl.BlockSpec((None, 1, mod.shape[-1]), lambda i: (i // per_b, 0, 0)),
                  pl.BlockSpec(fg.shape, lambda i: (0, 0))],
        out_specs=row(d),
        out_shape=jax.ShapeDtypeStruct((t, d), F32),
        compiler_params=_cparams(("arbitrary",)),
        name="moe_combine",
    )(x2, y_tiles, y_tiles, gts, mod, fg)


def _pick(n, pref):
    while n % pref:
        pref //= 2
    return pref


@jax.jit
def kernel(x, c, norm_mix_g, ada_mix_w, ada_mix_b, w_in, hg_lb_logits, hg_norm_g, ml_conv_w, ml_conv_b, ml_wq,
           ml_wk, ml_fbias, ml_norm_g, w_br_hg, w_br_ml, w_out, norm_ffn_g, ada_ffn_w, ada_ffn_b, ffn_w1,
           ffn_w3, ffn_w2, moe_router_w, moe_router_b, moe_w1, moe_w3, moe_w2, final_norm_g):
    batch, seq, d = x.shape
    depth = w_in.shape[0]
    t = batch * seq
    tm = _pick(seq, 512)
    ts_hg = _pick(seq, 512)
    ts_ml = _pick(seq, ML_GROUP * ML_CHUNK)

    mod_mix, mod_ffn = _ada_params(c, ada_mix_w, ada_mix_b, ada_ffn_w, ada_ffn_b)
    mod_mix = mod_mix.reshape(depth, batch, 1, 3 * d)
    mod_ffn = mod_ffn.reshape(depth, batch, 1, 3 * d)

    lb_all = jnp.cumsum(jax.nn.softmax(hg_lb_logits.astype(F32), axis=0), axis=0)
    lb_all = lb_all - lb_all[0:1]

    n_main = 4 * HG_W + 3 * ML_W
    x2 = x.reshape(t, d)
    for l in range(depth):
        wl = w_in[l]
        wm = wl[:, :n_main].astype(BF16)
        wif = jnp.pad(wl[:, n_main:n_main + 2 * ML_HEADS], ((0, 0), (0, LANES - 2 * ML_HEADS))).astype(BF16)
        wg = wl[:, n_main + 2 * ML_HEADS:].astype(BF16)
        lb = lb_all[l][None, :]
        lbp = jnp.concatenate([jnp.log(lb), jnp.log1p(-lb), 1.0 - lb, jnp.zeros((5, HG_W), F32)], axis=0)
        pad_qk = lambda w: jnp.pad(w, ((0, 0), (0, 0), (0, ML_DV - ML_DQK))).astype(BF16)
        q, k, v, gs, lf, mq, mk, mv, mo, ga, gb, mif = _proj(
            x2, mod_mix[l], norm_mix_g[l][None, :], wm, wg, wif, lbp, ml_conv_w[l], ml_conv_b[l][None, :],
            pad_qk(ml_wq[l] * (ML_DQK ** -0.5)), pad_qk(ml_wk[l]), seq, tm)

        yh = _hgrn2(q, k, v, lf, gs, hg_norm_g[l][None, :], batch, seq, ts_hg)

        fb = jnp.zeros((1, LANES), F32).at[0, ML_HEADS:2 * ML_HEADS].set(ml_fbias[l])
        ym = _mlstm(mq, mk, mv, mo, mif, fb, ml_norm_g[l][None, :], batch, seq, ts_ml)

        j = l // 2
        routed = l % 2 == 1
        router = None
        if routed:
            router = (jnp.pad(moe_router_w[j], ((0, 0), (0, LANES - N_EXPERTS))).astype(BF16),
                      jnp.pad(moe_router_b[j], (0, LANES - N_EXPERTS))[None, :])
        merged = _merge(yh, ym, ga, gb, x2, mod_mix[l], mod_ffn[l], norm_ffn_g[l][None, :],
                        w_br_hg[l].astype(BF16), w_br_ml[l].astype(BF16), w_out[l].astype(BF16),
                        router, seq, tm)

        final = l == depth - 1
        fg = final_norm_g[None, :]
        if routed:
            x2, h2, ridx, gts = merged
            y = _moe(h2, _route_plan(ridx, t, tm), moe_w1[j].astype(BF16), moe_w3[j].astype(BF16),
                     moe_w2[j].astype(BF16), tm)
            x2 = _combine(x2, y, gts, mod_ffn[l], fg, seq, tm, final)
        else:
            x2, h2 = merged
            x2 = _ffn(x2, h2, mod_ffn[l], ffn_w1[j].astype(BF16), ffn_w3[j].astype(BF16),
                      ffn_w2[j].astype(BF16), fg, seq, tm, FF_TILE, final)
    return x2.reshape(batch, seq, d)
```

```python
import functools

import jax
import jax.numpy as jnp
from jax import lax
from jax.experimental import pallas as pl
from jax.experimental.pallas import tpu as pltpu

F32 = jnp.float32
BF16 = jnp.bfloat16

EPS = 1e-6
LANES = 128
ISSUE_UNROLL = 8
HG_HEADS = 4
HG_D = 128
HG_W = HG_HEADS * HG_D
ML_HEADS = 4
ML_DV = 128
ML_DQK = 64
ML_W = ML_HEADS * ML_DV
CONV_K = 4
N_EXPERTS = 8
HG_CHUNK = 64
HG_SUB = 8
HG_GROUP = 8
ML_CHUNK = 256
ML_GROUP = 2
LOG2E = 1.4426950408889634
FF_TILE = 1408
VMEM_LIMIT = 56 * 1024 * 1024


def _cparams(sem):
    return pltpu.CompilerParams(dimension_semantics=sem, vmem_limit_bytes=VMEM_LIMIT)


def _const_spec(shape):
    nd = len(shape)
    return pl.BlockSpec(shape, lambda *_: (0,) * nd, pipeline_mode=pl.Buffered(1))


def _aligned(start, multiple):
    return start if isinstance(start, int) else pl.multiple_of(start, multiple)


def _sigmoid(v):
    return 0.5 * jnp.tanh(0.5 * v) + 0.5


def _silu(v):
    return v * _sigmoid(v)


def _log_sigmoid(z):
    return jnp.minimum(z, 0.0) - jnp.log1p(jnp.exp(-jnp.abs(z)))


def _split_bf16(v):
    hi = v.astype(BF16)
    lo = (v - hi.astype(F32)).astype(BF16)
    return hi, lo


def _dot(a, b):
    return jnp.dot(a, b, preferred_element_type=F32)


def _dot_nt(a, b):
    return lax.dot_general(a, b, (((1,), (1,)), ((), ())), preferred_element_type=F32)


def _dot_tn(a, b):
    return lax.dot_general(a, b, (((0,), (0,)), ((), ())), preferred_element_type=F32)


def _ada_kernel(c_ref, wm_ref, bm_ref, wf_ref, bf_ref, om_ref, of_ref):
    c = c_ref[...]
    ca = _silu(c).astype(BF16)
    om_ref[...] = _dot(ca, wm_ref[...].astype(BF16)) + bm_ref[...]
    of_ref[...] = _dot(ca, wf_ref[...].astype(BF16)) + bf_ref[...]


def _ada_params(c, wm, bm, wf, bf):
    depth, d, d3 = wm.shape
    b = c.shape[0]
    nt = d3 // d
    w_spec = pl.BlockSpec((None, d, d), lambda l, j: (l, 0, j))
    b_spec = pl.BlockSpec((None, 1, d), lambda l, j: (l, 0, j))
    o_spec = pl.BlockSpec((None, b, d), lambda l, j: (l, 0, j))
    out = jax.ShapeDtypeStruct((depth, b, d3), F32)
    return pl.pallas_call(
        _ada_kernel,
        grid=(depth, nt),
        in_specs=[pl.BlockSpec((b, d), lambda l, j: (0, 0)), w_spec, b_spec, w_spec, b_spec],
        out_specs=[o_spec, o_spec],
        out_shape=[out, out],
        compiler_params=_cparams(("arbitrary", "arbitrary")),
        name="ada_params",
    )(c, wm, bm.reshape(depth, 1, d3), wf, bf.reshape(depth, 1, d3))


def _norm_modulate(x, g, shift, scale):
    ms = jnp.mean(x * x, axis=-1, keepdims=True)
    y = x * lax.rsqrt(ms + EPS) * g
    return y * (1.0 + scale) + shift


def _proj_kernel(x_ref, mod_ref, g_ref, wm_ref, wg_ref, wif_ref, lbp_ref, cw_ref, cb_ref, wq_ref, wk_ref,
                 q_ref, k_ref, v_ref, gs_ref, lf_ref, mq_ref, mk_ref, mv_ref, mo_ref, ga_ref, gb_ref, mif_ref,
                 carry_ref, *, per_b):
    tm, d = x_ref.shape
    mod = mod_ref[...]
    h = _norm_modulate(x_ref[...], g_ref[...], mod[:, :d], mod[:, d:2 * d]).astype(BF16)

    def piece(w_ref, j, width=HG_W):
        return _dot(h, w_ref[:, j * width:(j + 1) * width])

    q_ref[...] = _silu(piece(wm_ref, 0)).astype(BF16)
    zf = piece(wm_ref, 1)
    log_lb, log1m_lb, one_m_lb = lbp_ref[0:1, :], lbp_ref[1:2, :], lbp_ref[2:3, :]
    t = log1m_lb + _log_sigmoid(zf)
    lf_ref[...] = jnp.maximum(log_lb, t) + jnp.log1p(jnp.exp(-jnp.abs(log_lb - t)))
    k_ref[...] = (one_m_lb * _sigmoid(-zf)).astype(BF16)
    v_ref[...] = piece(wm_ref, 2).astype(BF16)
    gs_ref[...] = _silu(piece(wm_ref, 3)).astype(BF16)
    @pl.when(pl.program_id(0) % per_b == 0)
    def _():
        carry_ref[...] = jnp.zeros_like(carry_ref)

    u = piece(wm_ref, 4)
    ext = jnp.concatenate([carry_ref[...], u], axis=0)
    carry_ref[...] = u[tm - 8:, :]
    conv = cb_ref[...]
    for j in range(CONV_K):
        off = 8 - (CONV_K - 1) + j
        conv = conv + cw_ref[j:j + 1, :] * ext[off:off + tm, :]
    uc = _silu(conv).astype(BF16)
    for hh in range(ML_HEADS):
        cols = slice(hh * ML_DV, (hh + 1) * ML_DV)
        mq_ref[:, cols] = _dot(uc[:, cols], wq_ref[hh]).astype(BF16)
        mk_ref[:, cols] = _dot(uc[:, cols], wk_ref[hh]).astype(BF16)
    mv_ref[...] = piece(wm_ref, 5).astype(BF16)
    mo_ref[...] = _sigmoid(piece(wm_ref, 6)).astype(BF16)
    ga_ref[...] = _sigmoid(piece(wg_ref, 0, d)).astype(BF16)
    gb_ref[...] = _sigmoid(piece(wg_ref, 1, d)).astype(BF16)
    mif_ref[...] = _dot(h, wif_ref[...])


def _proj(x2, mod, g, wm, wg, wif, lbp, cw, cb, wq, wk, seq, tm):
    t, d = x2.shape
    per_b = seq // tm
    row = lambda w: pl.BlockSpec((tm, w), lambda i: (i, 0))
    sds = lambda w, dt: jax.ShapeDtypeStruct((t, w), dt)
    outs = [(HG_W, BF16)] * 4 + [(HG_W, F32)] + [(ML_W, BF16)] * 4 + [(d, BF16)] * 2 + [(LANES, F32)]
    consts = [g, wm, wg, wif, lbp, cw, cb, wq, wk]
    return pl.pallas_call(
        functools.partial(_proj_kernel, per_b=per_b),
        grid=(t // tm,),
        in_specs=[row(d), pl.BlockSpec((None, 1, mod.shape[-1]), lambda i: (i // per_b, 0, 0))]
                 + [_const_spec(a.shape) for a in consts],
        out_specs=[row(w) for w, _ in outs],
        out_shape=[sds(w, dt) for w, dt in outs],
        scratch_shapes=[pltpu.VMEM((8, ML_W), F32)],
        compiler_params=_cparams(("arbitrary",)),
        name="in_proj",
    )(x2, mod, g, *consts[1:])


def _head_norm(o, g):
    return o * lax.rsqrt(jnp.mean(o * o, axis=-1, keepdims=True) + EPS) * g


def _hgrn2_body(q_ref, k_ref, v_ref, lf_ref, gs_ref, g_ref, tri_ref, sel_ref, y_ref, st_ref):
    ts = q_ref.shape[0]
    nsub = HG_CHUNK // HG_SUB

    tri = tri_ref[...]
    row = lax.broadcasted_iota(jnp.int32, (HG_CHUNK, HG_CHUNK), 0)
    col = lax.broadcasted_iota(jnp.int32, (HG_CHUNK, HG_CHUNK), 1)
    diag_mask = (row // HG_SUB == col // HG_SUB) & (col <= row)

    def intra(c):
        rows = pl.ds(_aligned(c * HG_CHUNK, HG_CHUNK), HG_CHUNK)
        hi, lo = _split_bf16(lf_ref[rows, :])
        a_all = (_dot(tri, hi) + _dot(tri, lo)) * LOG2E
        heads = []
        for hh in range(HG_HEADS):
            cols = slice(hh * HG_D, (hh + 1) * HG_D)
            heads.append((a_all[:, cols], q_ref[rows, cols].astype(F32), k_ref[rows, cols].astype(F32),
                          v_ref[rows, cols], gs_ref[rows, cols]))
        p_heads = []
        for a, qf, kf, _, _ in heads:
            a4 = a.reshape(nsub, HG_SUB, HG_D)
            q4 = qf.reshape(nsub, HG_SUB, HG_D)
            k4 = kf.reshape(nsub, HG_SUB, HG_D)
            pieces = []
            for s in range(HG_SUB):
                dec = jnp.exp2(jnp.minimum(a4 - a4[:, s:s + 1, :], 0.0))
                pieces.append((q4 * k4[:, s:s + 1, :] * dec).reshape(HG_CHUNK, HG_D).astype(BF16))
            p_heads.append(jnp.concatenate(pieces, axis=1))
        diag = _dot(jnp.concatenate(p_heads, axis=0), sel_ref[...])
        out = []
        for hh, (a, qf, kf, vb, gs) in enumerate(heads):
            a_last = a[HG_CHUNK - 1:HG_CHUNK, :]
            qs, ks = [], []
            for j in range(nsub - 1):
                e = a[(j + 1) * HG_SUB - 1:(j + 1) * HG_SUB, :]
                lo_r, hi_r = j * HG_SUB, (j + 1) * HG_SUB
                kt = kf[lo_r:hi_r] * jnp.exp2(e - a[lo_r:hi_r])
                qt = qf[hi_r:] * jnp.exp2(a[hi_r:] - e)
                ks.append(jnp.concatenate(
                    [jnp.zeros((lo_r, HG_D), F32), kt, jnp.zeros((HG_CHUNK - hi_r, HG_D), F32)], axis=0)
                    if lo_r else jnp.concatenate([kt, jnp.zeros((HG_CHUNK - hi_r, HG_D), F32)], axis=0))
                qs.append(jnp.concatenate([jnp.zeros((hi_r, HG_D), F32), qt], axis=0))
            s_off = _dot_nt(jnp.concatenate(qs, axis=1).astype(BF16), jnp.concatenate(ks, axis=1).astype(BF16))
            s_diag = diag[hh * HG_CHUNK:(hh + 1) * HG_CHUNK, :HG_CHUNK]
            s_full = s_off + jnp.where(diag_mask, s_diag, 0.0)
            k_dec = (kf * jnp.exp2(a_last - a)).astype(BF16)
            out.append((_dot(s_full.astype(BF16), vb), (qf * jnp.exp2(a)).astype(BF16), jnp.exp2(a_last),
                        _dot_tn(vb, k_dec), gs))
        return out

    def group(gi, states):
        parts = [intra(gi * HG_GROUP + c) for c in range(HG_GROUP)]
        states = list(states)
        ys = []
        for part in parts:
            y_heads = []
            for hh, (o_intra, q_dec, decay, incr, gs) in enumerate(part):
                st = states[hh]
                o = o_intra + _dot_nt(q_dec, st.astype(BF16))
                states[hh] = st * decay + incr
                cols = slice(hh * HG_D, (hh + 1) * HG_D)
                y_heads.append((_head_norm(o, g_ref[:, cols]) * gs.astype(F32)).astype(BF16))
            ys.append(jnp.concatenate(y_heads, axis=1))
        for c, y in enumerate(ys):
            y_ref[pl.ds(_aligned((gi * HG_GROUP + c) * HG_CHUNK, HG_CHUNK), HG_CHUNK), :] = y
        return tuple(states)

    states = tuple(st_ref[hh] for hh in range(HG_HEADS))
    n_groups = ts // (HG_CHUNK * HG_GROUP)
    states = group(0, states) if n_groups == 1 else lax.fori_loop(0, n_groups, group, states)
    for hh in range(HG_HEADS):
        st_ref[hh] = states[hh]


def _mlstm_body(mq_ref, mk_ref, mv_ref, mo_ref, mif_ref, fb_ref, g_ref, tri_ref, y_ref, c_ref, m_ref):
    ts = mq_ref.shape[0]
    nh = ML_HEADS
    n_chunks = ts // ML_CHUNK

    tri = tri_ref[...]
    row = lax.broadcasted_iota(jnp.int32, (ML_CHUNK, ML_CHUNK), 0)
    col = lax.broadcasted_iota(jnp.int32, (ML_CHUNK, ML_CHUNK), 1)
    causal = col <= row
    ones_blk = jnp.ones((ML_CHUNK, ML_DV), BF16)
    lane = lax.broadcasted_iota(jnp.int32, (ML_CHUNK, LANES), 1)
    lane8 = lax.broadcasted_iota(jnp.int32, (2 * nh, ML_CHUNK), 1)

    m_prev = m_ref[...]
    gates = []
    for c in range(n_chunks):
        rows = slice(c * ML_CHUNK, (c + 1) * ML_CHUNK)
        mif = mif_ref[rows, :]
        hi, lo = _split_bf16(_log_sigmoid(mif + fb_ref[...]))
        bcum = _dot(tri, hi) + _dot(tri, lo)
        z = jnp.where(lane < nh, mif, bcum)
        b_last = z[ML_CHUNK - 1:ML_CHUNK, :]
        state_log = b_last - z + pltpu.roll(z, nh, axis=1)
        m_new = jnp.maximum(b_last + m_prev, jnp.max(state_log, axis=0, keepdims=True))
        gates.append((z, z + m_prev, jnp.exp(state_log - m_new), jnp.exp(b_last + m_prev - m_new)))
        m_prev = m_new
    m_ref[...] = m_prev

    parts = []
    for c, (z, inter_log, w_s, decay_prev) in enumerate(gates):
        rows = slice(c * ML_CHUNK, (c + 1) * ML_CHUNK)
        zt8 = z.T[0:2 * nh, :]
        r8 = pltpu.roll(zt8, nh, axis=0) - zt8
        cm = r8
        shift = 1
        while shift < ML_CHUNK:
            cm = jnp.maximum(cm, jnp.where(lane8 >= shift, pltpu.roll(cm, shift, axis=1), -jnp.inf))
            shift *= 2
        cm_cols = jnp.concatenate([cm, jnp.zeros((LANES - 2 * nh, ML_CHUNK), F32)], axis=0).T
        m_t = jnp.maximum(z + cm_cols, inter_log)
        w_inter = jnp.exp(inter_log - m_t)
        floor = jnp.exp(-m_t)
        g2 = (z - m_t) * LOG2E
        r2 = r8 * LOG2E
        heads = []
        for hh in range(nh):
            cols = slice(hh * ML_DV, (hh + 1) * ML_DV)
            sc = slice(nh + hh, nh + hh + 1)
            q = mq_ref[rows, cols]
            k = mk_ref[rows, cols]
            v_aug = jnp.concatenate([mv_ref[rows, cols], ones_blk], axis=1)
            w = jnp.where(causal, jnp.exp2(g2[:, sc] + r2[nh + hh:nh + hh + 1, :]), 0.0)
            p = (_dot_nt(q, k) * w).astype(BF16)
            heads.append((q, _dot(p, v_aug), w_inter[:, sc], floor[:, sc], decay_prev[:, sc],
                          _dot_tn((w_s[:, sc] * k.astype(F32)).astype(BF16), v_aug), mo_ref[rows, cols]))
        parts.append(heads)

    c_state = [c_ref[hh] for hh in range(nh)]
    ys = []
    for heads in parts:
        y_heads = []
        for hh, (q, pv, w_inter, floor, decay_prev, incr, mo) in enumerate(heads):
            cols = slice(hh * ML_DV, (hh + 1) * ML_DV)
            tot = pv + w_inter * _dot(q, c_state[hh].astype(BF16))
            c_state[hh] = decay_prev * c_state[hh] + incr
            hout = tot[:, :ML_DV] / jnp.maximum(jnp.abs(tot[:, ML_DV:]), floor)
            y_heads.append((mo.astype(F32) * _head_norm(hout, g_ref[:, cols])).astype(BF16))
        ys.append(jnp.concatenate(y_heads, axis=1))
    for c, y in enumerate(ys):
        y_ref[c * ML_CHUNK:(c + 1) * ML_CHUNK, :] = y
    for hh in range(nh):
        c_ref[hh] = c_state[hh]


def _mixers_kernel(q_ref, k_ref, v_ref, lf_ref, gs_ref, mq_ref, mk_ref, mv_ref, mo_ref, mif_ref,
                   hg_g_ref, tri_h_ref, sel_ref, fb_ref, ml_g_ref, tri_m_ref, yh_ref, ym_ref, st_ref, c_ref, m_ref):
    @pl.when(pl.program_id(1) == 0)
    def _():
        st_ref[...] = jnp.zeros_like(st_ref)
        c_ref[...] = jnp.zeros_like(c_ref)
        m_ref[...] = jnp.zeros_like(m_ref)

    _hgrn2_body(q_ref, k_ref, v_ref, lf_ref, gs_ref, hg_g_ref, tri_h_ref, sel_ref, yh_ref, st_ref)
    _mlstm_body(mq_ref, mk_ref, mv_ref, mo_ref, mif_ref, fb_ref, ml_g_ref, tri_m_ref, ym_ref, c_ref, m_ref)


def _mixers(q, k, v, lf, gs, mq, mk, mv, mo, mif, hg_g, fb, ml_g, batch, seq, ts):
    t = q.shape[0]
    per_b = seq // ts
    tri_h = jnp.tril(jnp.ones((HG_CHUNK, HG_CHUNK), F32)).astype(BF16)
    r = jnp.arange(HG_SUB * HG_D)[:, None] // HG_D
    cidx = jnp.arange(LANES)[None, :]
    sel = ((cidx % HG_SUB == r) & (cidx < HG_CHUNK)).astype(BF16)
    tri_m = jnp.tril(jnp.ones((ML_CHUNK, ML_CHUNK), F32)).astype(BF16)
    row = lambda w: pl.BlockSpec((ts, w), lambda b, s: (b * per_b + s, 0))
    consts = [hg_g, tri_h, sel, fb, ml_g, tri_m]
    out = jax.ShapeDtypeStruct((t, HG_W), BF16)
    return pl.pallas_call(
        _mixers_kernel,
        grid=(batch, per_b),
        in_specs=[row(HG_W)] * 5 + [row(ML_W)] * 4 + [row(LANES)] + [_const_spec(a.shape) for a in consts],
        out_specs=[row(HG_W), row(ML_W)],
        out_shape=[out, out],
        scratch_shapes=[pltpu.VMEM((HG_HEADS, HG_D, HG_D), F32),
                        pltpu.VMEM((ML_HEADS, ML_DV, 2 * ML_DV), F32), pltpu.VMEM((1, LANES), F32)],
        compiler_params=_cparams(("arbitrary", "arbitrary")),
        name="mixers",
    )(q, k, v, lf, gs, mq, mk, mv, mo, mif, *consts)


def _store_token_tiles(ref, base, val):
    n, d = val.shape
    nj = d // LANES
    for j in range(nj):
        ref[pl.ds(base + j, n, stride=nj), :] = val[:, j * LANES:(j + 1) * LANES]


def _load_token_tiles(ref, base, n, d):
    nj = d // LANES
    return [ref[pl.ds(base + j, n, stride=nj), :] for j in range(nj)]


def _merge_kernel(yh_ref, ym_ref, ga_ref, gb_ref, x_ref, modm_ref, modf_ref, g_ref, wbh_ref, wbm_ref, wo_ref,
                  *rest, routed):
    d = x_ref.shape[-1]
    merged = (ga_ref[...].astype(F32) * _dot(yh_ref[...], wbh_ref[...])
              + gb_ref[...].astype(F32) * _dot(ym_ref[...], wbm_ref[...]))
    mix = _dot(merged.astype(BF16), wo_ref[...])
    x_new = x_ref[...] + modm_ref[:, 2 * d:] * mix
    modf = modf_ref[...]
    h = _norm_modulate(x_new, g_ref[...], modf[:, :d], modf[:, d:2 * d])
    if not routed:
        xo_ref, h_ref = rest
        xo_ref[...] = x_new
        h_ref[...] = h.astype(BF16)
        return
    rw_ref, rb_ref, xo_ref, h_ref, ridx_ref, gts_ref = rest
    xo_ref[...] = x_new
    _store_token_tiles(h_ref, 0, h)
    logits = _dot(h.astype(BF16), rw_ref[...]) + rb_ref[...]
    lane = lax.broadcasted_iota(jnp.int32, logits.shape, 1)
    logits = jnp.where(lane < N_EXPERTS, logits, -jnp.inf)
    m1 = jnp.max(logits, axis=1, keepdims=True)
    i1 = jnp.min(jnp.where(logits == m1, lane, LANES), axis=1, keepdims=True)
    others = jnp.where(lane == i1, -jnp.inf, logits)
    m2 = jnp.max(others, axis=1, keepdims=True)
    i2 = jnp.min(jnp.where(others == m2, lane, LANES), axis=1, keepdims=True)
    e = jnp.exp(m2 - m1)
    g1 = 1.0 / (1.0 + e)
    ridx_ref[...] = jnp.where(lane == 0, i1, jnp.where(lane == 1, i2, 0))
    gts_ref[...] = jnp.where(lane == 0, g1, jnp.where(lane == 1, e * g1, 0.0))


def _merge(yh, ym, ga, gb, x2, modm, modf, g, wbh, wbm, wo, router, seq, tm):
    t, d = x2.shape
    per_b = seq // tm
    routed = router is not None
    row = lambda w: pl.BlockSpec((tm, w), lambda i: (i, 0))
    mod_spec = pl.BlockSpec((None, 1, modm.shape[-1]), lambda i: (i // per_b, 0, 0))
    consts = [g, wbh, wbm, wo] + (list(router) if routed else [])
    nj = d // LANES
    if routed:
        outs = [(t, tm, d, F32), (t * nj, tm * nj, LANES, F32), (t, tm, LANES, jnp.int32), (t, tm, LANES, F32)]
    else:
        outs = [(t, tm, d, F32), (t, tm, d, BF16)]
    return pl.pallas_call(
        functools.partial(_merge_kernel, routed=routed),
        grid=(t // tm,),
        in_specs=[row(HG_W), row(ML_W), row(d), row(d), row(d), mod_spec, mod_spec]
                 + [_const_spec(a.shape) for a in consts],
        out_specs=[pl.BlockSpec((br, w), lambda i: (i, 0)) for _, br, w, _ in outs],
        out_shape=[jax.ShapeDtypeStruct((n, w), dt) for n, _, w, dt in outs],
        compiler_params=_cparams(("arbitrary",)),
        name="merge_out_proj_routed" if routed else "merge_out_proj",
    )(yh, ym, ga, gb, x2, modm, modf, *consts)


def _final_norm(out, fg_ref):
    return out * lax.rsqrt(jnp.mean(out * out, axis=-1, keepdims=True) + EPS) * fg_ref[...]


def _ffn_kernel(x_ref, h_ref, mod_ref, w1_ref, w3_ref, w2_ref, fg_ref, o_ref, acc_ref, *, final_norm):
    f = pl.program_id(1)
    d = x_ref.shape[-1]

    @pl.when(f == 0)
    def _():
        acc_ref[...] = jnp.zeros_like(acc_ref)

    h = h_ref[...]
    act = (_silu(_dot(h, w1_ref[...])) * _dot(h, w3_ref[...])).astype(BF16)
    acc_ref[...] += _dot(act, w2_ref[...])

    @pl.when(f == pl.num_programs(1) - 1)
    def _():
        out = x_ref[...] + mod_ref[:, 2 * d:] * acc_ref[...]
        o_ref[...] = _final_norm(out, fg_ref) if final_norm else out


def _ffn(x2, h, mod, w1, w3, w2, fg, seq, tm, tf, final_norm):
    t, d = x2.shape
    dff = w1.shape[-1]
    per_b = seq // tm
    row = lambda w: pl.BlockSpec((tm, w), lambda i, f: (i, 0))
    return pl.pallas_call(
        functools.partial(_ffn_kernel, final_norm=final_norm),
        grid=(t // tm, dff // tf),
        in_specs=[row(d), row(d),
                  pl.BlockSpec((None, 1, mod.shape[-1]), lambda i, f: (i // per_b, 0, 0)),
                  pl.BlockSpec((d, tf), lambda i, f: (0, f)),
                  pl.BlockSpec((d, tf), lambda i, f: (0, f)),
                  pl.BlockSpec((tf, d), lambda i, f: (f, 0)),
                  pl.BlockSpec(fg.shape, lambda i, f: (0, 0))],
        out_specs=row(d),
        out_shape=jax.ShapeDtypeStruct((t, d), F32),
        scratch_shapes=[pltpu.VMEM((tm, d), F32)],
        compiler_params=_cparams(("arbitrary", "arbitrary")),
        name="swiglu_dense",
    )(x2, h, mod, w1, w3, w2, fg)


def _route_plan(ridx, t, tm):
    n_assign = 2 * t
    n_tiles = n_assign // tm + N_EXPERTS
    e_flat = jnp.concatenate([ridx[:, 0], ridx[:, 1]])
    order = jnp.sort(e_flat * n_assign + jnp.arange(n_assign, dtype=jnp.int32)) % n_assign
    counts = jnp.sum((e_flat[:, None] == jnp.arange(N_EXPERTS, dtype=jnp.int32)[None, :]).astype(jnp.int32), axis=0)
    tiles_e = (counts + tm - 1) // tm
    tile_end = jnp.cumsum(tiles_e)
    n_used = tile_end[-1]
    seg_start = jnp.cumsum(counts) - counts
    ti = jnp.arange(n_tiles, dtype=jnp.int32)
    te = jnp.minimum(jnp.sum((ti[:, None] >= tile_end[None, :]).astype(jnp.int32), axis=1), N_EXPERTS - 1)
    j = ti - (tile_end - tiles_e)[te]
    n_valid = jnp.where(ti < n_used, jnp.clip(counts[te] - j * tm, 0, tm), 0)
    r = jnp.arange(tm, dtype=jnp.int32)[None, :]
    valid = r < n_valid[:, None]
    a = order[jnp.clip(seg_start[te][:, None] + j[:, None] * tm + r, 0, n_assign - 1)]
    gather_tok = jnp.where(valid, a % t, 0)
    scatter_row = jnp.where(valid, a, n_assign + (ti % 2)[:, None] * tm + r)
    scatter_row = jnp.concatenate([n_assign + tm + r, scatter_row], axis=0)
    tile_expert = jnp.where(ti < n_used, te, te[jnp.maximum(n_used - 1, 0)])
    return (gather_tok.reshape(n_tiles, 1, tm), scatter_row.reshape(n_tiles + 1, 1, tm),
            tile_expert.astype(jnp.int32), n_used.reshape(1).astype(jnp.int32))


def _moe_kernel(te_ref, nu_ref, gcur_ref, gnext_ref, sprev_ref, scur_ref, h_hbm, w1_ref, w3_ref, w2_ref, y_hbm,
                xbuf, ybuf, gsem, ssem):
    i = pl.program_id(0)
    n_used = nu_ref[0]
    d = w1_ref.shape[0]
    nj = d // LANES
    tm = xbuf.shape[0] // (2 * nj)
    slot = i % 2
    other = 1 - slot

    def rows(ref, r, n=1):
        return ref.at[pl.ds(pl.multiple_of(r * nj, nj), n * nj)]

    def gather_row(idx_ref, sl, r):
        return pltpu.make_async_copy(rows(h_hbm, idx_ref[0, r]), rows(xbuf, sl * tm + r), gsem.at[sl])

    def scatter_row(idx_ref, sl, r):
        return pltpu.make_async_copy(rows(ybuf, sl * tm + r), rows(y_hbm, idx_ref[0, r]), ssem.at[sl])

    def wait_slot(buf, sem, sl):
        whole = rows(buf, sl * tm, tm)
        pltpu.make_async_copy(whole, whole, sem.at[sl]).wait()

    def issue_rows(*copies):
        def body(rb, carry):
            for k in range(ISSUE_UNROLL):
                for make in copies:
                    make(rb * ISSUE_UNROLL + k).start()
            return carry
        lax.fori_loop(0, tm // ISSUE_UNROLL, body, 0)

    @pl.when(i == 0)
    def _():
        issue_rows(lambda r: gather_row(gcur_ref, 0, r))
        ybuf[pl.ds(tm * nj, tm * nj), :] = jnp.zeros((tm * nj, LANES), F32)
        for half in range(2):
            fill = pltpu.make_async_copy(rows(ybuf, tm, tm), rows(y_hbm, y_hbm.shape[0] // nj - (2 - half) * tm, tm),
                                         ssem.at[1])
            fill.start()
            fill.wait()

    @pl.when(i < n_used)
    def _():
        @pl.when(i >= 1)
        def _():
            wait_slot(ybuf, ssem, slot)

        wait_slot(xbuf, gsem, slot)
        issue_rows(lambda r: gather_row(gnext_ref, other, r), lambda r: scatter_row(sprev_ref, other, r))

        base = pl.multiple_of(slot * (tm * nj), tm * nj)
        x = jnp.concatenate(_load_token_tiles(xbuf, base, tm, d), axis=1).astype(BF16)
        act = (_silu(_dot(x, w1_ref[...])) * _dot(x, w3_ref[...])).astype(BF16)
        _store_token_tiles(ybuf, base, _dot(act, w2_ref[...]))

        @pl.when(i == n_used - 1)
        def _():
            issue_rows(lambda r: scatter_row(scur_ref, slot, r))
            wait_slot(ybuf, ssem, other)
            wait_slot(ybuf, ssem, slot)
            wait_slot(xbuf, gsem, other)


def _moe(h_tiles, plan, w1, w3, w2, tm):
    gather_tok, scatter_row, tile_expert, n_used = plan
    n_tiles = gather_tok.shape[0]
    _, d, dff = w1.shape
    nj = d // LANES
    t = h_tiles.shape[0] // nj
    idx_spec = lambda off, last: pl.BlockSpec(
        (None, 1, tm), lambda i, te, nu: (jnp.minimum(i + off, last), 0, 0), memory_space=pltpu.SMEM)
    buf = pltpu.VMEM((2 * tm * nj, LANES), F32)
    grid_spec = pltpu.PrefetchScalarGridSpec(
        num_scalar_prefetch=2,
        grid=(n_tiles,),
        in_specs=[idx_spec(0, n_tiles - 1), idx_spec(1, n_tiles - 1), idx_spec(0, n_tiles), idx_spec(1, n_tiles),
                  pl.BlockSpec(memory_space=pl.ANY),
                  pl.BlockSpec((None, d, dff), lambda i, te, nu: (te[i], 0, 0)),
                  pl.BlockSpec((None, d, dff), lambda i, te, nu: (te[i], 0, 0)),
                  pl.BlockSpec((None, dff, d), lambda i, te, nu: (te[i], 0, 0))],
        out_specs=pl.BlockSpec(memory_space=pl.ANY),
        scratch_shapes=[buf, buf, pltpu.SemaphoreType.DMA((2,)), pltpu.SemaphoreType.DMA((2,))],
    )
    return pl.pallas_call(
        _moe_kernel,
        grid_spec=grid_spec,
        out_shape=jax.ShapeDtypeStruct(((2 * t + 2 * tm) * nj, LANES), F32),
        compiler_params=_cparams(("arbitrary",)),
        name="moe_experts",
    )(tile_expert, n_used, gather_tok, gather_tok, scatter_row, scatter_row, h_tiles, w1, w3, w2)


def _combine_kernel(x_ref, y0_ref, y1_ref, gts_ref, mod_ref, fg_ref, o_ref, *, final_norm):
    tm, d = x_ref.shape
    gts = gts_ref[...]
    ff = jnp.concatenate(
        [gts[:, 0:1] * a + gts[:, 1:2] * b
         for a, b in zip(_load_token_tiles(y0_ref, 0, tm, d), _load_token_tiles(y1_ref, 0, tm, d))], axis=1)
    out = x_ref[...] + mod_ref[:, 2 * d:] * ff
    o_ref[...] = _final_norm(out, fg_ref) if final_norm else out


def _combine(x2, y_tiles, gts, mod, fg, seq, tm, final_norm):
    t, d = x2.shape
    nj = d // LANES
    per_b = seq // tm
    nb = t // tm
    row = lambda w: pl.BlockSpec((tm, w), lambda i: (i, 0))
    return pl.pallas_call(
        functools.partial(_combine_kernel, final_norm=final_norm),
        grid=(nb,),
        in_specs=[row(d), pl.BlockSpec((tm * nj, LANES), lambda i: (i, 0)),
                  pl.BlockSpec((tm * nj, LANES), lambda i: (nb + i, 0)), row(LANES),
                  pl.BlockSpec((None, 1, mod.shape[-1]), lambda i: (i // per_b, 0, 0)),
                  pl.BlockSpec(fg.shape, lambda i: (0, 0))],
        out_specs=row(d),
        out_shape=jax.ShapeDtypeStruct((t, d), F32),
        compiler_params=_cparams(("arbitrary",)),
        name="moe_combine",
    )(x2, y_tiles, y_tiles, gts, mod, fg)


def _pick(n, pref):
    while n % pref:
        pref //= 2
    return pref


@jax.jit
def kernel(x, c, norm_mix_g, ada_mix_w, ada_mix_b, w_in, hg_lb_logits, hg_norm_g, ml_conv_w, ml_conv_b, ml_wq,
           ml_wk, ml_fbias, ml_norm_g, w_br_hg, w_br_ml, w_out, norm_ffn_g, ada_ffn_w, ada_ffn_b, ffn_w1,
           ffn_w3, ffn_w2, moe_router_w, moe_router_b, moe_w1, moe_w3, moe_w2, final_norm_g):
    batch, seq, d = x.shape
    depth = w_in.shape[0]
    t = batch * seq
    tm = _pick(seq, 512)
    ts_mix = _pick(seq, ML_GROUP * ML_CHUNK)

    mod_mix, mod_ffn = _ada_params(c, ada_mix_w, ada_mix_b, ada_ffn_w, ada_ffn_b)
    mod_mix = mod_mix.reshape(depth, batch, 1, 3 * d)
    mod_ffn = mod_ffn.reshape(depth, batch, 1, 3 * d)

    lb_all = jnp.cumsum(jax.nn.softmax(hg_lb_logits.astype(F32), axis=0), axis=0)
    lb_all = lb_all - lb_all[0:1]

    n_main = 4 * HG_W + 3 * ML_W
    x2 = x.reshape(t, d)
    for l in range(depth):
        wl = w_in[l]
        wm = wl[:, :n_main].astype(BF16)
        wif = jnp.pad(wl[:, n_main:n_main + 2 * ML_HEADS], ((0, 0), (0, LANES - 2 * ML_HEADS))).astype(BF16)
        wg = wl[:, n_main + 2 * ML_HEADS:].astype(BF16)
        lb = lb_all[l][None, :]
        lbp = jnp.concatenate([jnp.log(lb), jnp.log1p(-lb), 1.0 - lb, jnp.zeros((5, HG_W), F32)], axis=0)
        pad_qk = lambda w: jnp.pad(w, ((0, 0), (0, 0), (0, ML_DV - ML_DQK))).astype(BF16)
        q, k, v, gs, lf, mq, mk, mv, mo, ga, gb, mif = _proj(
            x2, mod_mix[l], norm_mix_g[l][None, :], wm, wg, wif, lbp, ml_conv_w[l], ml_conv_b[l][None, :],
            pad_qk(ml_wq[l] * (ML_DQK ** -0.5)), pad_qk(ml_wk[l]), seq, tm)

        fb = jnp.zeros((1, LANES), F32).at[0, ML_HEADS:2 * ML_HEADS].set(ml_fbias[l])
        yh, ym = _mixers(q, k, v, lf, gs, mq, mk, mv, mo, mif, hg_norm_g[l][None, :], fb,
                         ml_norm_g[l][None, :], batch, seq, ts_mix)

        j = l // 2
        routed = l % 2 == 1
        router = None
        if routed:
            router = (jnp.pad(moe_router_w[j], ((0, 0), (0, LANES - N_EXPERTS))).astype(BF16),
                      jnp.pad(moe_router_b[j], (0, LANES - N_EXPERTS))[None, :])
        merged = _merge(yh, ym, ga, gb, x2, mod_mix[l], mod_ffn[l], norm_ffn_g[l][None, :],
                        w_br_hg[l].astype(BF16), w_br_ml[l].astype(BF16), w_out[l].astype(BF16),
                        router, seq, tm)

        final = l == depth - 1
        fg = final_norm_g[None, :]
        if routed:
            x2, h2, ridx, gts = merged
            y = _moe(h2, _route_plan(ridx, t, tm), moe_w1[j].astype(BF16), moe_w3[j].astype(BF16),
                     moe_w2[j].astype(BF16), tm)
            x2 = _combine(x2, y, gts, mod_ffn[l], fg, seq, tm, final)
        else:
            x2, h2 = merged
            x2 = _ffn(x2, h2, mod_ffn[l], ffn_w1[j].astype(BF16), ffn_w3[j].astype(BF16),
                      ffn_w2[j].astype(BF16), fg, seq, tm, FF_TILE, final)
    return x2.reshape(batch, seq, d)
```

```python
import functools

import jax
import jax.numpy as jnp
from jax import lax
from jax.experimental import pallas as pl
from jax.experimental.pallas import tpu as pltpu

F32 = jnp.float32
BF16 = jnp.bfloat16

EPS = 1e-6
LANES = 128
ISSUE_UNROLL = 8
PROJ_AHEAD = 1
HG_HEADS = 4
HG_D = 128
HG_W = HG_HEADS * HG_D
ML_HEADS = 4
ML_DV = 128
ML_DQK = 64
ML_W = ML_HEADS * ML_DV
CONV_K = 4
N_EXPERTS = 8
HG_CHUNK = 64
HG_SUB = 8
HG_GROUP = 8
ML_CHUNK = 256
ML_GROUP = 2
LOG2E = 1.4426950408889634
FF_TILE = 1408
VMEM_LIMIT = 56 * 1024 * 1024


def _cparams(sem):
    return pltpu.CompilerParams(dimension_semantics=sem, vmem_limit_bytes=VMEM_LIMIT)


def _const_spec(shape):
    nd = len(shape)
    return pl.BlockSpec(shape, lambda *_: (0,) * nd, pipeline_mode=pl.Buffered(1))


def _aligned(start, multiple):
    return start if isinstance(start, int) else pl.multiple_of(start, multiple)


def _sigmoid(v):
    return 0.5 * jnp.tanh(0.5 * v) + 0.5


def _silu(v):
    return v * _sigmoid(v)


def _log_sigmoid(z):
    return jnp.minimum(z, 0.0) - jnp.log1p(jnp.exp(-jnp.abs(z)))


def _split_bf16(v):
    hi = v.astype(BF16)
    lo = (v - hi.astype(F32)).astype(BF16)
    return hi, lo


def _dot(a, b):
    return jnp.dot(a, b, preferred_element_type=F32)


def _dot_nt(a, b):
    return lax.dot_general(a, b, (((1,), (1,)), ((), ())), preferred_element_type=F32)


def _dot_tn(a, b):
    return lax.dot_general(a, b, (((0,), (0,)), ((), ())), preferred_element_type=F32)


def _ada_kernel(c_ref, wm_ref, bm_ref, wf_ref, bf_ref, om_ref, of_ref):
    c = c_ref[...]
    ca = _silu(c).astype(BF16)
    om_ref[...] = _dot(ca, wm_ref[...].astype(BF16)) + bm_ref[...]
    of_ref[...] = _dot(ca, wf_ref[...].astype(BF16)) + bf_ref[...]


def _ada_params(c, wm, bm, wf, bf):
    depth, d, d3 = wm.shape
    b = c.shape[0]
    nt = d3 // d
    w_spec = pl.BlockSpec((None, d, d), lambda l, j: (l, 0, j))
    b_spec = pl.BlockSpec((None, 1, d), lambda l, j: (l, 0, j))
    o_spec = pl.BlockSpec((None, b, d), lambda l, j: (l, 0, j))
    out = jax.ShapeDtypeStruct((depth, b, d3), F32)
    return pl.pallas_call(
        _ada_kernel,
        grid=(depth, nt),
        in_specs=[pl.BlockSpec((b, d), lambda l, j: (0, 0)), w_spec, b_spec, w_spec, b_spec],
        out_specs=[o_spec, o_spec],
        out_shape=[out, out],
        compiler_params=_cparams(("arbitrary", "arbitrary")),
        name="ada_params",
    )(c, wm, bm.reshape(depth, 1, d3), wf, bf.reshape(depth, 1, d3))


def _norm_modulate(x, g, shift, scale):
    ms = jnp.mean(x * x, axis=-1, keepdims=True)
    y = x * lax.rsqrt(ms + EPS) * g
    return y * (1.0 + scale) + shift


def _proj_kernel(x_ref, mod_ref, g_ref, wm_ref, wg_ref, wif_ref, lbp_ref, cw_ref, cb_ref, wq_ref, wk_ref,
                 q_ref, k_ref, v_ref, gs_ref, lf_ref, mq_ref, mk_ref, mv_ref, mo_ref, ga_ref, gb_ref, mif_ref,
                 carry_ref, *, per_b):
    tm, d = x_ref.shape
    mod = mod_ref[...]
    h = _norm_modulate(x_ref[...], g_ref[...], mod[:, :d], mod[:, d:2 * d]).astype(BF16)

    def piece(w_ref, j, width=HG_W):
        return _dot(h, w_ref[:, j * width:(j + 1) * width])

    @pl.when(pl.program_id(0) % per_b == 0)
    def _():
        carry_ref[...] = jnp.zeros_like(carry_ref)

    log_lb, log1m_lb, one_m_lb = lbp_ref[0:1, :], lbp_ref[1:2, :], lbp_ref[2:3, :]

    def forget_gate(zf):
        t = log1m_lb + _log_sigmoid(zf)
        lf_ref[...] = jnp.maximum(log_lb, t) + jnp.log1p(jnp.exp(-jnp.abs(log_lb - t)))
        k_ref[...] = (one_m_lb * _sigmoid(-zf)).astype(BF16)

    def mlstm_qk(u):
        ext = jnp.concatenate([carry_ref[...], u], axis=0)
        carry_ref[...] = u[tm - 8:, :]
        conv = cb_ref[...]
        for j in range(CONV_K):
            off = 8 - (CONV_K - 1) + j
            conv = conv + cw_ref[j:j + 1, :] * ext[off:off + tm, :]
        uc = _silu(conv).astype(BF16)
        for hh in range(ML_HEADS):
            cols = slice(hh * ML_DV, (hh + 1) * ML_DV)
            mq_ref[:, cols] = _dot(uc[:, cols], wq_ref[hh]).astype(BF16)
            mk_ref[:, cols] = _dot(uc[:, cols], wk_ref[hh]).astype(BF16)

    def store(ref, fn=None):
        def epilogue(val):
            ref[...] = (val if fn is None else fn(val)).astype(ref.dtype)
        return epilogue

    stages = [
        (lambda: piece(wm_ref, 1), forget_gate),
        (lambda: piece(wg_ref, 0, d), store(ga_ref, _sigmoid)),
        (lambda: piece(wm_ref, 4), mlstm_qk),
        (lambda: piece(wg_ref, 1, d), store(gb_ref, _sigmoid)),
        (lambda: piece(wm_ref, 0), store(q_ref, _silu)),
        (lambda: piece(wm_ref, 2), store(v_ref)),
        (lambda: piece(wm_ref, 3), store(gs_ref, _silu)),
        (lambda: piece(wm_ref, 5), store(mv_ref)),
        (lambda: piece(wm_ref, 6), store(mo_ref, _sigmoid)),
        (lambda: _dot(h, wif_ref[...]), store(mif_ref)),
    ]
    pending = []
    for project, epilogue in stages:
        pending.append((project(), epilogue))
        if len(pending) > PROJ_AHEAD:
            val, done = pending.pop(0)
            done(val)
    for val, done in pending:
        done(val)


def _proj(x2, mod, g, wm, wg, wif, lbp, cw, cb, wq, wk, seq, tm):
    t, d = x2.shape
    per_b = seq // tm
    row = lambda w: pl.BlockSpec((tm, w), lambda i: (i, 0))
    sds = lambda w, dt: jax.ShapeDtypeStruct((t, w), dt)
    outs = [(HG_W, BF16)] * 4 + [(HG_W, F32)] + [(ML_W, BF16)] * 4 + [(d, BF16)] * 2 + [(LANES, F32)]
    consts = [g, wm, wg, wif, lbp, cw, cb, wq, wk]
    return pl.pallas_call(
        functools.partial(_proj_kernel, per_b=per_b),
        grid=(t // tm,),
        in_specs=[row(d), pl.BlockSpec((None, 1, mod.shape[-1]), lambda i: (i // per_b, 0, 0))]
                 + [_const_spec(a.shape) for a in consts],
        out_specs=[row(w) for w, _ in outs],
        out_shape=[sds(w, dt) for w, dt in outs],
        scratch_shapes=[pltpu.VMEM((8, ML_W), F32)],
        compiler_params=_cparams(("arbitrary",)),
        name="in_proj",
    )(x2, mod, g, *consts[1:])


def _head_norm(o, g):
    return o * lax.rsqrt(jnp.mean(o * o, axis=-1, keepdims=True) + EPS) * g


def _hgrn2_body(q_ref, k_ref, v_ref, lf_ref, gs_ref, g_ref, tri_ref, sel_ref, y_ref, st_ref):
    ts = q_ref.shape[0]
    nsub = HG_CHUNK // HG_SUB

    tri = tri_ref[...]
    row = lax.broadcasted_iota(jnp.int32, (HG_CHUNK, HG_CHUNK), 0)
    col = lax.broadcasted_iota(jnp.int32, (HG_CHUNK, HG_CHUNK), 1)
    diag_mask = (row // HG_SUB == col // HG_SUB) & (col <= row)

    def intra(c):
        rows = pl.ds(_aligned(c * HG_CHUNK, HG_CHUNK), HG_CHUNK)
        hi, lo = _split_bf16(lf_ref[rows, :])
        a_all = (_dot(tri, hi) + _dot(tri, lo)) * LOG2E
        heads = []
        for hh in range(HG_HEADS):
            cols = slice(hh * HG_D, (hh + 1) * HG_D)
            heads.append((a_all[:, cols], q_ref[rows, cols].astype(F32), k_ref[rows, cols].astype(F32),
                          v_ref[rows, cols], gs_ref[rows, cols]))
        p_heads = []
        for a, qf, kf, _, _ in heads:
            a4 = a.reshape(nsub, HG_SUB, HG_D)
            q4 = qf.reshape(nsub, HG_SUB, HG_D)
            k4 = kf.reshape(nsub, HG_SUB, HG_D)
            pieces = []
            for s in range(HG_SUB):
                dec = jnp.exp2(jnp.minimum(a4 - a4[:, s:s + 1, :], 0.0))
                pieces.append((q4 * k4[:, s:s + 1, :] * dec).reshape(HG_CHUNK, HG_D).astype(BF16))
            p_heads.append(jnp.concatenate(pieces, axis=1))
        diag = _dot(jnp.concatenate(p_heads, axis=0), sel_ref[...])
        out = []
        for hh, (a, qf, kf, vb, gs) in enumerate(heads):
            a_last = a[HG_CHUNK - 1:HG_CHUNK, :]
            qs, ks = [], []
            for j in range(nsub - 1):
                e = a[(j + 1) * HG_SUB - 1:(j + 1) * HG_SUB, :]
                lo_r, hi_r = j * HG_SUB, (j + 1) * HG_SUB
                kt = kf[lo_r:hi_r] * jnp.exp2(e - a[lo_r:hi_r])
                qt = qf[hi_r:] * jnp.exp2(a[hi_r:] - e)
                ks.append(jnp.concatenate(
                    [jnp.zeros((lo_r, HG_D), F32), kt, jnp.zeros((HG_CHUNK - hi_r, HG_D), F32)], axis=0)
                    if lo_r else jnp.concatenate([kt, jnp.zeros((HG_CHUNK - hi_r, HG_D), F32)], axis=0))
                qs.append(jnp.concatenate([jnp.zeros((hi_r, HG_D), F32), qt], axis=0))
            s_off = _dot_nt(jnp.concatenate(qs, axis=1).astype(BF16), jnp.concatenate(ks, axis=1).astype(BF16))
            s_diag = diag[hh * HG_CHUNK:(hh + 1) * HG_CHUNK, :HG_CHUNK]
            s_full = s_off + jnp.where(diag_mask, s_diag, 0.0)
            k_dec = (kf * jnp.exp2(a_last - a)).astype(BF16)
            out.append((_dot(s_full.astype(BF16), vb), (qf * jnp.exp2(a)).astype(BF16), jnp.exp2(a_last),
                        _dot_tn(vb, k_dec), gs))
        return out

    def group(gi, states):
        parts = [intra(gi * HG_GROUP + c) for c in range(HG_GROUP)]
        states = list(states)
        ys = []
        for part in parts:
            y_heads = []
            for hh, (o_intra, q_dec, decay, incr, gs) in enumerate(part):
                st = states[hh]
                o = o_intra + _dot_nt(q_dec, st.astype(BF16))
                states[hh] = st * decay + incr
                cols = slice(hh * HG_D, (hh + 1) * HG_D)
                y_heads.append((_head_norm(o, g_ref[:, cols]) * gs.astype(F32)).astype(BF16))
            ys.append(jnp.concatenate(y_heads, axis=1))
        for c, y in enumerate(ys):
            y_ref[pl.ds(_aligned((gi * HG_GROUP + c) * HG_CHUNK, HG_CHUNK), HG_CHUNK), :] = y
        return tuple(states)

    states = tuple(st_ref[hh] for hh in range(HG_HEADS))
    n_groups = ts // (HG_CHUNK * HG_GROUP)
    states = group(0, states) if n_groups == 1 else lax.fori_loop(0, n_groups, group, states)
    for hh in range(HG_HEADS):
        st_ref[hh] = states[hh]


def _mlstm_body(mq_ref, mk_ref, mv_ref, mo_ref, mif_ref, fb_ref, g_ref, tri_ref, y_ref, c_ref, m_ref):
    ts = mq_ref.shape[0]
    nh = ML_HEADS
    n_chunks = ts // ML_CHUNK

    tri = tri_ref[...]
    row = lax.broadcasted_iota(jnp.int32, (ML_CHUNK, ML_CHUNK), 0)
    col = lax.broadcasted_iota(jnp.int32, (ML_CHUNK, ML_CHUNK), 1)
    causal = col <= row
    ones_blk = jnp.ones((ML_CHUNK, ML_DV), BF16)
    lane = lax.broadcasted_iota(jnp.int32, (ML_CHUNK, LANES), 1)
    lane8 = lax.broadcasted_iota(jnp.int32, (2 * nh, ML_CHUNK), 1)

    m_prev = m_ref[...]
    gates = []
    for c in range(n_chunks):
        rows = slice(c * ML_CHUNK, (c + 1) * ML_CHUNK)
        mif = mif_ref[rows, :]
        hi, lo = _split_bf16(_log_sigmoid(mif + fb_ref[...]))
        bcum = _dot(tri, hi) + _dot(tri, lo)
        z = jnp.where(lane < nh, mif, bcum)
        b_last = z[ML_CHUNK - 1:ML_CHUNK, :]
        state_log = b_last - z + pltpu.roll(z, nh, axis=1)
        m_new = jnp.maximum(b_last + m_prev, jnp.max(state_log, axis=0, keepdims=True))
        gates.append((z, z + m_prev, jnp.exp(state_log - m_new), jnp.exp(b_last + m_prev - m_new)))
        m_prev = m_new
    m_ref[...] = m_prev

    parts = []
    for c, (z, inter_log, w_s, decay_prev) in enumerate(gates):
        rows = slice(c * ML_CHUNK, (c + 1) * ML_CHUNK)
        zt8 = z.T[0:2 * nh, :]
        r8 = pltpu.roll(zt8, nh, axis=0) - zt8
        cm = r8
        shift = 1
        while shift < ML_CHUNK:
            cm = jnp.maximum(cm, jnp.where(lane8 >= shift, pltpu.roll(cm, shift, axis=1), -jnp.inf))
            shift *= 2
        cm_cols = jnp.concatenate([cm, jnp.zeros((LANES - 2 * nh, ML_CHUNK), F32)], axis=0).T
        m_t = jnp.maximum(z + cm_cols, inter_log)
        w_inter = jnp.exp(inter_log - m_t)
        floor = jnp.exp(-m_t)
        g2 = (z - m_t) * LOG2E
        r2 = r8 * LOG2E
        heads = []
        for hh in range(nh):
            cols = slice(hh * ML_DV, (hh + 1) * ML_DV)
            sc = slice(nh + hh, nh + hh + 1)
            q = mq_ref[rows, cols]
            k = mk_ref[rows, cols]
            v_aug = jnp.concatenate([mv_ref[rows, cols], ones_blk], axis=1)
            w = jnp.where(causal, jnp.exp2(g2[:, sc] + r2[nh + hh:nh + hh + 1, :]), 0.0)
            p = (_dot_nt(q, k) * w).astype(BF16)
            heads.append((q, _dot(p, v_aug), w_inter[:, sc], floor[:, sc], decay_prev[:, sc],
                          _dot_tn((w_s[:, sc] * k.astype(F32)).astype(BF16), v_aug), mo_ref[rows, cols]))
        parts.append(heads)

    c_state = [c_ref[hh] for hh in range(nh)]
    ys = []
    for heads in parts:
        y_heads = []
        for hh, (q, pv, w_inter, floor, decay_prev, incr, mo) in enumerate(heads):
            cols = slice(hh * ML_DV, (hh + 1) * ML_DV)
            tot = pv + w_inter * _dot(q, c_state[hh].astype(BF16))
            c_state[hh] = decay_prev * c_state[hh] + incr
            hout = tot[:, :ML_DV] / jnp.maximum(jnp.abs(tot[:, ML_DV:]), floor)
            y_heads.append((mo.astype(F32) * _head_norm(hout, g_ref[:, cols])).astype(BF16))
        ys.append(jnp.concatenate(y_heads, axis=1))
    for c, y in enumerate(ys):
        y_ref[c * ML_CHUNK:(c + 1) * ML_CHUNK, :] = y
    for hh in range(nh):
        c_ref[hh] = c_state[hh]


def _mixers_kernel(q_ref, k_ref, v_ref, lf_ref, gs_ref, mq_ref, mk_ref, mv_ref, mo_ref, mif_ref,
                   hg_g_ref, tri_h_ref, sel_ref, fb_ref, ml_g_ref, tri_m_ref, yh_ref, ym_ref, st_ref, c_ref, m_ref):
    @pl.when(pl.program_id(1) == 0)
    def _():
        st_ref[...] = jnp.zeros_like(st_ref)
        c_ref[...] = jnp.zeros_like(c_ref)
        m_ref[...] = jnp.zeros_like(m_ref)

    _hgrn2_body(q_ref, k_ref, v_ref, lf_ref, gs_ref, hg_g_ref, tri_h_ref, sel_ref, yh_ref, st_ref)
    _mlstm_body(mq_ref, mk_ref, mv_ref, mo_ref, mif_ref, fb_ref, ml_g_ref, tri_m_ref, ym_ref, c_ref, m_ref)


def _mixers(q, k, v, lf, gs, mq, mk, mv, mo, mif, hg_g, fb, ml_g, batch, seq, ts):
    t = q.shape[0]
    per_b = seq // ts
    tri_h = jnp.tril(jnp.ones((HG_CHUNK, HG_CHUNK), F32)).astype(BF16)
    r = jnp.arange(HG_SUB * HG_D)[:, None] // HG_D
    cidx = jnp.arange(LANES)[None, :]
    sel = ((cidx % HG_SUB == r) & (cidx < HG_CHUNK)).astype(BF16)
    tri_m = jnp.tril(jnp.ones((ML_CHUNK, ML_CHUNK), F32)).astype(BF16)
    row = lambda w: pl.BlockSpec((ts, w), lambda b, s: (b * per_b + s, 0))
    consts = [hg_g, tri_h, sel, fb, ml_g, tri_m]
    out = jax.ShapeDtypeStruct((t, HG_W), BF16)
    return pl.pallas_call(
        _mixers_kernel,
        grid=(batch, per_b),
        in_specs=[row(HG_W)] * 5 + [row(ML_W)] * 4 + [row(LANES)] + [_const_spec(a.shape) for a in consts],
        out_specs=[row(HG_W), row(ML_W)],
        out_shape=[out, out],
        scratch_shapes=[pltpu.VMEM((HG_HEADS, HG_D, HG_D), F32),
                        pltpu.VMEM((ML_HEADS, ML_DV, 2 * ML_DV), F32), pltpu.VMEM((1, LANES), F32)],
        compiler_params=_cparams(("arbitrary", "arbitrary")),
        name="mixers",
    )(q, k, v, lf, gs, mq, mk, mv, mo, mif, *consts)


def _store_token_tiles(ref, base, val):
    n, d = val.shape
    nj = d // LANES
    for j in range(nj):
        ref[pl.ds(base + j, n, stride=nj), :] = val[:, j * LANES:(j + 1) * LANES]


def _load_token_tiles(ref, base, n, d):
    nj = d // LANES
    return [ref[pl.ds(base + j, n, stride=nj), :] for j in range(nj)]


def _merge_kernel(yh_ref, ym_ref, ga_ref, gb_ref, x_ref, modm_ref, modf_ref, g_ref, wbh_ref, wbm_ref, wo_ref,
                  *rest, routed):
    d = x_ref.shape[-1]
    merged = (ga_ref[...].astype(F32) * _dot(yh_ref[...], wbh_ref[...])
              + gb_ref[...].astype(F32) * _dot(ym_ref[...], wbm_ref[...]))
    mix = _dot(merged.astype(BF16), wo_ref[...])
    x_new = x_ref[...] + modm_ref[:, 2 * d:] * mix
    modf = modf_ref[...]
    h = _norm_modulate(x_new, g_ref[...], modf[:, :d], modf[:, d:2 * d])
    if not routed:
        xo_ref, h_ref = rest
        xo_ref[...] = x_new
        h_ref[...] = h.astype(BF16)
        return
    rw_ref, rb_ref, xo_ref, h_ref, ridx_ref, gts_ref = rest
    xo_ref[...] = x_new
    _store_token_tiles(h_ref, 0, h)
    logits = _dot(h.astype(BF16), rw_ref[...]) + rb_ref[...]
    lane = lax.broadcasted_iota(jnp.int32, logits.shape, 1)
    logits = jnp.where(lane < N_EXPERTS, logits, -jnp.inf)
    m1 = jnp.max(logits, axis=1, keepdims=True)
    i1 = jnp.min(jnp.where(logits == m1, lane, LANES), axis=1, keepdims=True)
    others = jnp.where(lane == i1, -jnp.inf, logits)
    m2 = jnp.max(others, axis=1, keepdims=True)
    i2 = jnp.min(jnp.where(others == m2, lane, LANES), axis=1, keepdims=True)
    e = jnp.exp(m2 - m1)
    g1 = 1.0 / (1.0 + e)
    ridx_ref[...] = jnp.where(lane == 0, i1, jnp.where(lane == 1, i2, 0))
    gts_ref[...] = jnp.where(lane == 0, g1, jnp.where(lane == 1, e * g1, 0.0))


def _merge(yh, ym, ga, gb, x2, modm, modf, g, wbh, wbm, wo, router, seq, tm):
    t, d = x2.shape
    per_b = seq // tm
    routed = router is not None
    row = lambda w: pl.BlockSpec((tm, w), lambda i: (i, 0))
    mod_spec = pl.BlockSpec((None, 1, modm.shape[-1]), lambda i: (i // per_b, 0, 0))
    consts = [g, wbh, wbm, wo] + (list(router) if routed else [])
    nj = d // LANES
    if routed:
        outs = [(t, tm, d, F32), (t * nj, tm * nj, LANES, F32), (t, tm, LANES, jnp.int32), (t, tm, LANES, F32)]
    else:
        outs = [(t, tm, d, F32), (t, tm, d, BF16)]
    return pl.pallas_call(
        functools.partial(_merge_kernel, routed=routed),
        grid=(t // tm,),
        in_specs=[row(HG_W), row(ML_W), row(d), row(d), row(d), mod_spec, mod_spec]
                 + [_const_spec(a.shape) for a in consts],
        out_specs=[pl.BlockSpec((br, w), lambda i: (i, 0)) for _, br, w, _ in outs],
        out_shape=[jax.ShapeDtypeStruct((n, w), dt) for n, _, w, dt in outs],
        compiler_params=_cparams(("arbitrary",)),
        name="merge_out_proj_routed" if routed else "merge_out_proj",
    )(yh, ym, ga, gb, x2, modm, modf, *consts)


def _final_norm(out, fg_ref):
    return out * lax.rsqrt(jnp.mean(out * out, axis=-1, keepdims=True) + EPS) * fg_ref[...]


def _ffn_kernel(x_ref, h_ref, mod_ref, w1_ref, w3_ref, w2_ref, fg_ref, o_ref, acc_ref, *, final_norm):
    f = pl.program_id(1)
    d = x_ref.shape[-1]

    @pl.when(f == 0)
    def _():
        acc_ref[...] = jnp.zeros_like(acc_ref)

    h = h_ref[...]
    act = (_silu(_dot(h, w1_ref[...])) * _dot(h, w3_ref[...])).astype(BF16)
    acc_ref[...] += _dot(act, w2_ref[...])

    @pl.when(f == pl.num_programs(1) - 1)
    def _():
        out = x_ref[...] + mod_ref[:, 2 * d:] * acc_ref[...]
        o_ref[...] = _final_norm(out, fg_ref) if final_norm else out


def _ffn(x2, h, mod, w1, w3, w2, fg, seq, tm, tf, final_norm):
    t, d = x2.shape
    dff = w1.shape[-1]
    per_b = seq // tm
    row = lambda w: pl.BlockSpec((tm, w), lambda i, f: (i, 0))
    return pl.pallas_call(
        functools.partial(_ffn_kernel, final_norm=final_norm),
        grid=(t // tm, dff // tf),
        in_specs=[row(d), row(d),
                  pl.BlockSpec((None, 1, mod.shape[-1]), lambda i, f: (i // per_b, 0, 0)),
                  pl.BlockSpec((d, tf), lambda i, f: (0, f)),
                  pl.BlockSpec((d, tf), lambda i, f: (0, f)),
                  pl.BlockSpec((tf, d), lambda i, f: (f, 0)),
                  pl.BlockSpec(fg.shape, lambda i, f: (0, 0))],
        out_specs=row(d),
        out_shape=jax.ShapeDtypeStruct((t, d), F32),
        scratch_shapes=[pltpu.VMEM((tm, d), F32)],
        compiler_params=_cparams(("arbitrary", "arbitrary")),
        name="swiglu_dense",
    )(x2, h, mod, w1, w3, w2, fg)


def _route_plan(ridx, t, tm):
    n_assign = 2 * t
    n_tiles = n_assign // tm + N_EXPERTS
    e_flat = jnp.concatenate([ridx[:, 0], ridx[:, 1]])
    order = jnp.sort(e_flat * n_assign + jnp.arange(n_assign, dtype=jnp.int32)) % n_assign
    counts = jnp.sum((e_flat[:, None] == jnp.arange(N_EXPERTS, dtype=jnp.int32)[None, :]).astype(jnp.int32), axis=0)
    tiles_e = (counts + tm - 1) // tm
    tile_end = jnp.cumsum(tiles_e)
    n_used = tile_end[-1]
    seg_start = jnp.cumsum(counts) - counts
    ti = jnp.arange(n_tiles, dtype=jnp.int32)
    te = jnp.minimum(jnp.sum((ti[:, None] >= tile_end[None, :]).astype(jnp.int32), axis=1), N_EXPERTS - 1)
    j = ti - (tile_end - tiles_e)[te]
    n_valid = jnp.where(ti < n_used, jnp.clip(counts[te] - j * tm, 0, tm), 0)
    r = jnp.arange(tm, dtype=jnp.int32)[None, :]
    valid = r < n_valid[:, None]
    a = order[jnp.clip(seg_start[te][:, None] + j[:, None] * tm + r, 0, n_assign - 1)]
    gather_tok = jnp.where(valid, a % t, 0)
    scatter_row = jnp.where(valid, a, n_assign + (ti % 2)[:, None] * tm + r)
    scatter_row = jnp.concatenate([n_assign + tm + r, scatter_row], axis=0)
    tile_expert = jnp.where(ti < n_used, te, te[jnp.maximum(n_used - 1, 0)])
    return (gather_tok.reshape(n_tiles, 1, tm), scatter_row.reshape(n_tiles + 1, 1, tm),
            tile_expert.astype(jnp.int32), n_used.reshape(1).astype(jnp.int32))


def _moe_kernel(te_ref, nu_ref, gcur_ref, gnext_ref, sprev_ref, scur_ref, h_hbm, w1_ref, w3_ref, w2_ref, y_hbm,
                xbuf, ybuf, gsem, ssem):
    i = pl.program_id(0)
    n_used = nu_ref[0]
    d = w1_ref.shape[0]
    nj = d // LANES
    tm = xbuf.shape[0] // (2 * nj)
    slot = i % 2
    other = 1 - slot

    def rows(ref, r, n=1):
        return ref.at[pl.ds(pl.multiple_of(r * nj, nj), n * nj)]

    def gather_row(idx_ref, sl, r):
        return pltpu.make_async_copy(rows(h_hbm, idx_ref[0, r]), rows(xbuf, sl * tm + r), gsem.at[sl])

    def scatter_row(idx_ref, sl, r):
        return pltpu.make_async_copy(rows(ybuf, sl * tm + r), rows(y_hbm, idx_ref[0, r]), ssem.at[sl])

    def wait_slot(buf, sem, sl):
        whole = rows(buf, sl * tm, tm)
        pltpu.make_async_copy(whole, whole, sem.at[sl]).wait()

    def issue_rows(*copies):
        def body(rb, carry):
            for k in range(ISSUE_UNROLL):
                for make in copies:
                    make(rb * ISSUE_UNROLL + k).start()
            return carry
        lax.fori_loop(0, tm // ISSUE_UNROLL, body, 0)

    @pl.when(i == 0)
    def _():
        issue_rows(lambda r: gather_row(gcur_ref, 0, r))
        ybuf[pl.ds(tm * nj, tm * nj), :] = jnp.zeros((tm * nj, LANES), F32)
        for half in range(2):
            fill = pltpu.make_async_copy(rows(ybuf, tm, tm), rows(y_hbm, y_hbm.shape[0] // nj - (2 - half) * tm, tm),
                                         ssem.at[1])
            fill.start()
            fill.wait()

    @pl.when(i < n_used)
    def _():
        @pl.when(i >= 1)
        def _():
            wait_slot(ybuf, ssem, slot)

        wait_slot(xbuf, gsem, slot)
        issue_rows(lambda r: gather_row(gnext_ref, other, r), lambda r: scatter_row(sprev_ref, other, r))

        base = pl.multiple_of(slot * (tm * nj), tm * nj)
        x = jnp.concatenate(_load_token_tiles(xbuf, base, tm, d), axis=1).astype(BF16)
        act = (_silu(_dot(x, w1_ref[...])) * _dot(x, w3_ref[...])).astype(BF16)
        _store_token_tiles(ybuf, base, _dot(act, w2_ref[...]))

        @pl.when(i == n_used - 1)
        def _():
            issue_rows(lambda r: scatter_row(scur_ref, slot, r))
            wait_slot(ybuf, ssem, other)
            wait_slot(ybuf, ssem, slot)
            wait_slot(xbuf, gsem, other)


def _moe(h_tiles, plan, w1, w3, w2, tm):
    gather_tok, scatter_row, tile_expert, n_used = plan
    n_tiles = gather_tok.shape[0]
    _, d, dff = w1.shape
    nj = d // LANES
    t = h_tiles.shape[0] // nj
    idx_spec = lambda off, last: pl.BlockSpec(
        (None, 1, tm), lambda i, te, nu: (jnp.minimum(i + off, last), 0, 0), memory_space=pltpu.SMEM)
    buf = pltpu.VMEM((2 * tm * nj, LANES), F32)
    grid_spec = pltpu.PrefetchScalarGridSpec(
        num_scalar_prefetch=2,
        grid=(n_tiles,),
        in_specs=[idx_spec(0, n_tiles - 1), idx_spec(1, n_tiles - 1), idx_spec(0, n_tiles), idx_spec(1, n_tiles),
                  pl.BlockSpec(memory_space=pl.ANY),
                  pl.BlockSpec((None, d, dff), lambda i, te, nu: (te[i], 0, 0)),
                  pl.BlockSpec((None, d, dff), lambda i, te, nu: (te[i], 0, 0)),
                  pl.BlockSpec((None, dff, d), lambda i, te, nu: (te[i], 0, 0))],
        out_specs=pl.BlockSpec(memory_space=pl.ANY),
        scratch_shapes=[buf, buf, pltpu.SemaphoreType.DMA((2,)), pltpu.SemaphoreType.DMA((2,))],
    )
    return pl.pallas_call(
        _moe_kernel,
        grid_spec=grid_spec,
        out_shape=jax.ShapeDtypeStruct(((2 * t + 2 * tm) * nj, LANES), F32),
        compiler_params=_cparams(("arbitrary",)),
        name="moe_experts",
    )(tile_expert, n_used, gather_tok, gather_tok, scatter_row, scatter_row, h_tiles, w1, w3, w2)


def _combine_kernel(x_ref, y0_ref, y1_ref, gts_ref, mod_ref, fg_ref, o_ref, *, final_norm):
    tm, d = x_ref.shape
    gts = gts_ref[...]
    ff = jnp.concatenate(
        [gts[:, 0:1] * a + gts[:, 1:2] * b
         for a, b in zip(_load_token_tiles(y0_ref, 0, tm, d), _load_token_tiles(y1_ref, 0, tm, d))], axis=1)
    out = x_ref[...] + mod_ref[:, 2 * d:] * ff
    o_ref[...] = _final_norm(out, fg_ref) if final_norm else out


def _combine(x2, y_tiles, gts, mod, fg, seq, tm, final_norm):
    t, d = x2.shape
    nj = d // LANES
    per_b = seq // tm
    nb = t // tm
    row = lambda w: pl.BlockSpec((tm, w), lambda i: (i, 0))
    return pl.pallas_call(
        functools.partial(_combine_kernel, final_norm=final_norm),
        grid=(nb,),
        in_specs=[row(d), pl.BlockSpec((tm * nj, LANES), lambda i: (i, 0)),
                  pl.BlockSpec((tm * nj, LANES), lambda i: (nb + i, 0)), row(LANES),
                  pl.BlockSpec((None, 1, mod.shape[-1]), lambda i: (i // per_b, 0, 0)),
                  pl.BlockSpec(fg.shape, lambda i: (0, 0))],
        out_specs=row(d),
        out_shape=jax.ShapeDtypeStruct((t, d), F32),
        compiler_params=_cparams(("arbitrary",)),
        name="moe_combine",
    )(x2, y_tiles, y_tiles, gts, mod, fg)


def _pick(n, pref):
    while n % pref:
        pref //= 2
    return pref


@jax.jit
def kernel(x, c, norm_mix_g, ada_mix_w, ada_mix_b, w_in, hg_lb_logits, hg_norm_g, ml_conv_w, ml_conv_b, ml_wq,
           ml_wk, ml_fbias, ml_norm_g, w_br_hg, w_br_ml, w_out, norm_ffn_g, ada_ffn_w, ada_ffn_b, ffn_w1,
           ffn_w3, ffn_w2, moe_router_w, moe_router_b, moe_w1, moe_w3, moe_w2, final_norm_g):
    batch, seq, d = x.shape
    depth = w_in.shape[0]
    t = batch * seq
    tm = _pick(seq, 512)
    ts_mix = _pick(seq, ML_GROUP * ML_CHUNK)

    mod_mix, mod_ffn = _ada_params(c, ada_mix_w, ada_mix_b, ada_ffn_w, ada_ffn_b)
    mod_mix = mod_mix.reshape(depth, batch, 1, 3 * d)
    mod_ffn = mod_ffn.reshape(depth, batch, 1, 3 * d)

    lb_all = jnp.cumsum(jax.nn.softmax(hg_lb_logits.astype(F32), axis=0), axis=0)
    lb_all = lb_all - lb_all[0:1]

    n_main = 4 * HG_W + 3 * ML_W
    x2 = x.reshape(t, d)
    for l in range(depth):
        wl = w_in[l]
        wm = wl[:, :n_main].astype(BF16)
        wif = jnp.pad(wl[:, n_main:n_main + 2 * ML_HEADS], ((0, 0), (0, LANES - 2 * ML_HEADS))).astype(BF16)
        wg = wl[:, n_main + 2 * ML_HEADS:].astype(BF16)
        lb = lb_all[l][None, :]
        lbp = jnp.concatenate([jnp.log(lb), jnp.log1p(-lb), 1.0 - lb, jnp.zeros((5, HG_W), F32)], axis=0)
        pad_qk = lambda w: jnp.pad(w, ((0, 0), (0, 0), (0, ML_DV - ML_DQK))).astype(BF16)
        q, k, v, gs, lf, mq, mk, mv, mo, ga, gb, mif = _proj(
            x2, mod_mix[l], norm_mix_g[l][None, :], wm, wg, wif, lbp, ml_conv_w[l], ml_conv_b[l][None, :],
            pad_qk(ml_wq[l] * (ML_DQK ** -0.5)), pad_qk(ml_wk[l]), seq, tm)

        fb = jnp.zeros((1, LANES), F32).at[0, ML_HEADS:2 * ML_HEADS].set(ml_fbias[l])
        yh, ym = _mixers(q, k, v, lf, gs, mq, mk, mv, mo, mif, hg_norm_g[l][None, :], fb,
                         ml_norm_g[l][None, :], batch, seq, ts_mix)

        j = l // 2
        routed = l % 2 == 1
        router = None
        if routed:
            router = (jnp.pad(moe_router_w[j], ((0, 0), (0, LANES - N_EXPERTS))).astype(BF16),
                      jnp.pad(moe_router_b[j], (0, LANES - N_EXPERTS))[None, :])
        merged = _merge(yh, ym, ga, gb, x2, mod_mix[l], mod_ffn[l], norm_ffn_g[l][None, :],
                        w_br_hg[l].astype(BF16), w_br_ml[l].astype(BF16), w_out[l].astype(BF16),
                        router, seq, tm)

        final = l == depth - 1
        fg = final_norm_g[None, :]
        if routed:
            x2, h2, ridx, gts = merged
            y = _moe(h2, _route_plan(ridx, t, tm), moe_w1[j].astype(BF16), moe_w3[j].astype(BF16),
                     moe_w2[j].astype(BF16), tm)
            x2 = _combine(x2, y, gts, mod_ffn[l], fg, seq, tm, final)
        else:
            x2, h2 = merged
            x2 = _ffn(x2, h2, mod_ffn[l], ffn_w1[j].astype(BF16), ffn_w3[j].astype(BF16),
                      ffn_w2[j].astype(BF16), fg, seq, tm, FF_TILE, final)
    return x2.reshape(batch, seq, d)
```

```python
import functools

import jax
import jax.numpy as jnp
from jax import lax
from jax.experimental import pallas as pl
from jax.experimental.pallas import tpu as pltpu

F32 = jnp.float32
BF16 = jnp.bfloat16

EPS = 1e-6
LANES = 128
ISSUE_UNROLL = 8
PROJ_AHEAD = 1
HG_HEADS = 4
HG_D = 128
HG_W = HG_HEADS * HG_D
ML_HEADS = 4
ML_DV = 128
ML_DQK = 64
ML_W = ML_HEADS * ML_DV
CONV_K = 4
N_EXPERTS = 8
HG_CHUNK = 64
HG_SUB = 8
HG_GROUP = 8
ML_CHUNK = 256
ML_GROUP = 2
LOG2E = 1.4426950408889634
FF_TILE = 1408
VMEM_LIMIT = 56 * 1024 * 1024


def _cparams(sem):
    return pltpu.CompilerParams(dimension_semantics=sem, vmem_limit_bytes=VMEM_LIMIT)


def _const_spec(shape):
    nd = len(shape)
    return pl.BlockSpec(shape, lambda *_: (0,) * nd, pipeline_mode=pl.Buffered(1))


def _aligned(start, multiple):
    return start if isinstance(start, int) else pl.multiple_of(start, multiple)


def _sigmoid(v):
    return 0.5 * jnp.tanh(0.5 * v) + 0.5


def _silu(v):
    return v * _sigmoid(v)


def _log_sigmoid(z):
    return jnp.minimum(z, 0.0) - jnp.log1p(jnp.exp(-jnp.abs(z)))


def _split_bf16(v):
    hi = v.astype(BF16)
    lo = (v - hi.astype(F32)).astype(BF16)
    return hi, lo


def _dot(a, b):
    return jnp.dot(a, b, preferred_element_type=F32)


def _dot_nt(a, b):
    return lax.dot_general(a, b, (((1,), (1,)), ((), ())), preferred_element_type=F32)


def _dot_tn(a, b):
    return lax.dot_general(a, b, (((0,), (0,)), ((), ())), preferred_element_type=F32)


def _ada_kernel(c_ref, wm_ref, bm_ref, wf_ref, bf_ref, om_ref, of_ref):
    c = c_ref[...]
    ca = _silu(c).astype(BF16)
    om_ref[...] = _dot(ca, wm_ref[...].astype(BF16)) + bm_ref[...]
    of_ref[...] = _dot(ca, wf_ref[...].astype(BF16)) + bf_ref[...]


def _ada_params(c, wm, bm, wf, bf):
    depth, d, d3 = wm.shape
    b = c.shape[0]
    nt = d3 // d
    w_spec = pl.BlockSpec((None, d, d), lambda l, j: (l, 0, j))
    b_spec = pl.BlockSpec((None, 1, d), lambda l, j: (l, 0, j))
    o_spec = pl.BlockSpec((None, b, d), lambda l, j: (l, 0, j))
    out = jax.ShapeDtypeStruct((depth, b, d3), F32)
    return pl.pallas_call(
        _ada_kernel,
        grid=(depth, nt),
        in_specs=[pl.BlockSpec((b, d), lambda l, j: (0, 0)), w_spec, b_spec, w_spec, b_spec],
        out_specs=[o_spec, o_spec],
        out_shape=[out, out],
        compiler_params=_cparams(("arbitrary", "arbitrary")),
        name="ada_params",
    )(c, wm, bm.reshape(depth, 1, d3), wf, bf.reshape(depth, 1, d3))


def _norm_modulate(x, g, shift, scale):
    ms = jnp.mean(x * x, axis=-1, keepdims=True)
    y = x * lax.rsqrt(ms + EPS) * g
    return y * (1.0 + scale) + shift


def _proj_kernel(x_ref, mod_ref, g_ref, wm_ref, wg_ref, wif_ref, lbp_ref, cw_ref, cb_ref, wq_ref, wk_ref,
                 q_ref, k_ref, v_ref, gs_ref, lf_ref, mq_ref, mk_ref, mv_ref, mo_ref, ga_ref, gb_ref, mif_ref,
                 carry_ref, *, per_b):
    tm, d = x_ref.shape
    mod = mod_ref[...]
    h = _norm_modulate(x_ref[...], g_ref[...], mod[:, :d], mod[:, d:2 * d]).astype(BF16)

    def piece(w_ref, j, width=HG_W):
        return _dot(h, w_ref[:, j * width:(j + 1) * width])

    @pl.when(pl.program_id(0) % per_b == 0)
    def _():
        carry_ref[...] = jnp.zeros_like(carry_ref)

    log_lb, log1m_lb, one_m_lb = lbp_ref[0:1, :], lbp_ref[1:2, :], lbp_ref[2:3, :]

    def forget_gate(zf):
        t = log1m_lb + _log_sigmoid(zf)
        lf_ref[...] = jnp.maximum(log_lb, t) + jnp.log1p(jnp.exp(-jnp.abs(log_lb - t)))
        k_ref[...] = (one_m_lb * _sigmoid(-zf)).astype(BF16)

    def mlstm_qk(u):
        ext = jnp.concatenate([carry_ref[...], u], axis=0)
        carry_ref[...] = u[tm - 8:, :]
        conv = cb_ref[...]
        for j in range(CONV_K):
            off = 8 - (CONV_K - 1) + j
            conv = conv + cw_ref[j:j + 1, :] * ext[off:off + tm, :]
        uc = _silu(conv).astype(BF16)
        for hh in range(ML_HEADS):
            cols = slice(hh * ML_DV, (hh + 1) * ML_DV)
            mq_ref[:, cols] = _dot(uc[:, cols], wq_ref[hh]).astype(BF16)
            mk_ref[:, cols] = _dot(uc[:, cols], wk_ref[hh]).astype(BF16)

    def store(ref, fn=None):
        def epilogue(val):
            ref[...] = (val if fn is None else fn(val)).astype(ref.dtype)
        return epilogue

    stages = [
        (lambda: piece(wm_ref, 1), forget_gate),
        (lambda: piece(wg_ref, 0, d), store(ga_ref, _sigmoid)),
        (lambda: piece(wm_ref, 4), mlstm_qk),
        (lambda: piece(wg_ref, 1, d), store(gb_ref, _sigmoid)),
        (lambda: piece(wm_ref, 0), store(q_ref, _silu)),
        (lambda: piece(wm_ref, 2), store(v_ref)),
        (lambda: piece(wm_ref, 3), store(gs_ref, _silu)),
        (lambda: piece(wm_ref, 5), store(mv_ref)),
        (lambda: piece(wm_ref, 6), store(mo_ref, _sigmoid)),
        (lambda: _dot(h, wif_ref[...]), store(mif_ref)),
    ]
    pending = []
    for project, epilogue in stages:
        pending.append((project(), epilogue))
        if len(pending) > PROJ_AHEAD:
            val, done = pending.pop(0)
            done(val)
    for val, done in pending:
        done(val)


def _proj(x2, mod, g, wm, wg, wif, lbp, cw, cb, wq, wk, seq, tm):
    t, d = x2.shape
    per_b = seq // tm
    row = lambda w: pl.BlockSpec((tm, w), lambda i: (i, 0))
    sds = lambda w, dt: jax.ShapeDtypeStruct((t, w), dt)
    outs = [(HG_W, BF16)] * 4 + [(HG_W, F32)] + [(ML_W, BF16)] * 4 + [(d, BF16)] * 2 + [(LANES, F32)]
    consts = [g, wm, wg, wif, lbp, cw, cb, wq, wk]
    return pl.pallas_call(
        functools.partial(_proj_kernel, per_b=per_b),
        grid=(t // tm,),
        in_specs=[row(d), pl.BlockSpec((None, 1, mod.shape[-1]), lambda i: (i // per_b, 0, 0))]
                 + [_const_spec(a.shape) for a in consts],
        out_specs=[row(w) for w, _ in outs],
        out_shape=[sds(w, dt) for w, dt in outs],
        scratch_shapes=[pltpu.VMEM((8, ML_W), F32)],
        compiler_params=_cparams(("arbitrary",)),
        name="in_proj",
    )(x2, mod, g, *consts[1:])


def _head_norm(o, g):
    return o * lax.rsqrt(jnp.mean(o * o, axis=-1, keepdims=True) + EPS) * g


def _hgrn2_body(q_ref, k_ref, v_ref, lf_ref, gs_ref, g_ref, tri_ref, sel_ref, y_ref, st_ref, a_scr, k_scr):
    ts = q_ref.shape[0]
    nsub = HG_CHUNK // HG_SUB

    tri = tri_ref[...]
    row = lax.broadcasted_iota(jnp.int32, (HG_CHUNK, HG_CHUNK), 0)
    col = lax.broadcasted_iota(jnp.int32, (HG_CHUNK, HG_CHUNK), 1)
    diag_mask = (row // HG_SUB == col // HG_SUB) & (col <= row)

    def intra_pairs(c):
        rows = pl.ds(_aligned(c * HG_CHUNK, HG_CHUNK), HG_CHUNK)
        hi, lo = _split_bf16(lf_ref[rows, :])
        a_all = (_dot(tri, hi) + _dot(tri, lo)) * LOG2E
        heads = []
        for hh in range(HG_HEADS):
            cols = slice(hh * HG_D, (hh + 1) * HG_D)
            heads.append((a_all[:, cols], q_ref[rows, cols].astype(F32), k_ref[rows, cols].astype(F32),
                          v_ref[rows, cols], gs_ref[rows, cols]))
        a_scr[rows, :] = a_all
        k_scr[rows, :] = k_ref[rows, :].astype(F32)
        p_heads = []
        for hh, (a, qf, _, _, _) in enumerate(heads):
            cols = slice(hh * HG_D, (hh + 1) * HG_D)
            pieces = [[] for _ in range(HG_SUB)]
            for blk in range(nsub):
                a_b = a[blk * HG_SUB:(blk + 1) * HG_SUB]
                q_b = qf[blk * HG_SUB:(blk + 1) * HG_SUB]
                for s in range(HG_SUB):
                    key = pl.ds(c * HG_CHUNK + blk * HG_SUB + s, 1)
                    dec = jnp.exp2(jnp.minimum(a_b - a_scr[key, cols], 0.0))
                    pieces[s].append(q_b * k_scr[key, cols] * dec)
            p_heads.append(jnp.concatenate([jnp.concatenate(p, axis=0).astype(BF16) for p in pieces], axis=1))
        return heads, _dot(jnp.concatenate(p_heads, axis=0), sel_ref[...])

    def intra_rest(heads, diag):
        out = []
        for hh, (a, qf, kf, vb, gs) in enumerate(heads):
            a_last = a[HG_CHUNK - 1:HG_CHUNK, :]
            qs, ks = [], []
            for j in range(nsub - 1):
                e = a[(j + 1) * HG_SUB - 1:(j + 1) * HG_SUB, :]
                lo_r, hi_r = j * HG_SUB, (j + 1) * HG_SUB
                kt = kf[lo_r:hi_r] * jnp.exp2(e - a[lo_r:hi_r])
                qt = qf[hi_r:] * jnp.exp2(a[hi_r:] - e)
                ks.append(jnp.concatenate(
                    [jnp.zeros((lo_r, HG_D), F32), kt, jnp.zeros((HG_CHUNK - hi_r, HG_D), F32)], axis=0)
                    if lo_r else jnp.concatenate([kt, jnp.zeros((HG_CHUNK - hi_r, HG_D), F32)], axis=0))
                qs.append(jnp.concatenate([jnp.zeros((hi_r, HG_D), F32), qt], axis=0))
            s_off = _dot_nt(jnp.concatenate(qs, axis=1).astype(BF16), jnp.concatenate(ks, axis=1).astype(BF16))
            s_diag = diag[hh * HG_CHUNK:(hh + 1) * HG_CHUNK, :HG_CHUNK]
            s_full = s_off + jnp.where(diag_mask, s_diag, 0.0)
            k_dec = (kf * jnp.exp2(a_last - a)).astype(BF16)
            out.append((_dot(s_full.astype(BF16), vb), (qf * jnp.exp2(a)).astype(BF16), jnp.exp2(a_last),
                        _dot_tn(vb, k_dec), gs))
        return out

    def group(gi, states):
        parts, ahead = [], None
        for c in range(HG_GROUP):
            nxt = intra_pairs(gi * HG_GROUP + c)
            if ahead is not None:
                parts.append(intra_rest(*ahead))
            ahead = nxt
        parts.append(intra_rest(*ahead))
        states = list(states)
        ys = []
        for part in parts:
            y_heads = []
            for hh, (o_intra, q_dec, decay, incr, gs) in enumerate(part):
                st = states[hh]
                o = o_intra + _dot_nt(q_dec, st.astype(BF16))
                states[hh] = st * decay + incr
                cols = slice(hh * HG_D, (hh + 1) * HG_D)
                y_heads.append((_head_norm(o, g_ref[:, cols]) * gs.astype(F32)).astype(BF16))
            ys.append(jnp.concatenate(y_heads, axis=1))
        for c, y in enumerate(ys):
            y_ref[pl.ds(_aligned((gi * HG_GROUP + c) * HG_CHUNK, HG_CHUNK), HG_CHUNK), :] = y
        return tuple(states)

    states = tuple(st_ref[hh] for hh in range(HG_HEADS))
    n_groups = ts // (HG_CHUNK * HG_GROUP)
    states = group(0, states) if n_groups == 1 else lax.fori_loop(0, n_groups, group, states)
    for hh in range(HG_HEADS):
        st_ref[hh] = states[hh]


def _mlstm_body(mq_ref, mk_ref, mv_ref, mo_ref, mif_ref, fb_ref, g_ref, tri_ref, y_ref, c_ref, m_ref):
    ts = mq_ref.shape[0]
    nh = ML_HEADS
    n_chunks = ts // ML_CHUNK

    tri = tri_ref[...]
    row = lax.broadcasted_iota(jnp.int32, (ML_CHUNK, ML_CHUNK), 0)
    col = lax.broadcasted_iota(jnp.int32, (ML_CHUNK, ML_CHUNK), 1)
    causal = col <= row
    ones_blk = jnp.ones((ML_CHUNK, ML_DV), BF16)
    lane = lax.broadcasted_iota(jnp.int32, (ML_CHUNK, LANES), 1)
    lane8 = lax.broadcasted_iota(jnp.int32, (2 * nh, ML_CHUNK), 1)

    m_prev = m_ref[...]
    gates = []
    for c in range(n_chunks):
        rows = slice(c * ML_CHUNK, (c + 1) * ML_CHUNK)
        mif = mif_ref[rows, :]
        hi, lo = _split_bf16(_log_sigmoid(mif + fb_ref[...]))
        bcum = _dot(tri, hi) + _dot(tri, lo)
        z = jnp.where(lane < nh, mif, bcum)
        b_last = z[ML_CHUNK - 1:ML_CHUNK, :]
        state_log = b_last - z + pltpu.roll(z, nh, axis=1)
        m_new = jnp.maximum(b_last + m_prev, jnp.max(state_log, axis=0, keepdims=True))
        gates.append((z, z + m_prev, jnp.exp(state_log - m_new), jnp.exp(b_last + m_prev - m_new)))
        m_prev = m_new
    m_ref[...] = m_prev

    parts = []
    for c, (z, inter_log, w_s, decay_prev) in enumerate(gates):
        rows = slice(c * ML_CHUNK, (c + 1) * ML_CHUNK)
        zt8 = z.T[0:2 * nh, :]
        r8 = pltpu.roll(zt8, nh, axis=0) - zt8
        cm = r8
        shift = 1
        while shift < ML_CHUNK:
            cm = jnp.maximum(cm, jnp.where(lane8 >= shift, pltpu.roll(cm, shift, axis=1), -jnp.inf))
            shift *= 2
        cm_cols = jnp.concatenate([cm, jnp.zeros((LANES - 2 * nh, ML_CHUNK), F32)], axis=0).T
        m_t = jnp.maximum(z + cm_cols, inter_log)
        w_inter = jnp.exp(inter_log - m_t)
        floor = jnp.exp(-m_t)
        g2 = (z - m_t) * LOG2E
        r2 = r8 * LOG2E
        heads = []
        for hh in range(nh):
            cols = slice(hh * ML_DV, (hh + 1) * ML_DV)
            sc = slice(nh + hh, nh + hh + 1)
            q = mq_ref[rows, cols]
            k = mk_ref[rows, cols]
            v_aug = jnp.concatenate([mv_ref[rows, cols], ones_blk], axis=1)
            w = jnp.where(causal, jnp.exp2(g2[:, sc] + r2[nh + hh:nh + hh + 1, :]), 0.0)
            p = (_dot_nt(q, k) * w).astype(BF16)
            heads.append((q, _dot(p, v_aug), w_inter[:, sc], floor[:, sc], decay_prev[:, sc],
                          _dot_tn((w_s[:, sc] * k.astype(F32)).astype(BF16), v_aug), mo_ref[rows, cols]))
        parts.append(heads)

    c_state = [c_ref[hh] for hh in range(nh)]
    ys = []
    for heads in parts:
        y_heads = []
        for hh, (q, pv, w_inter, floor, decay_prev, incr, mo) in enumerate(heads):
            cols = slice(hh * ML_DV, (hh + 1) * ML_DV)
            tot = pv + w_inter * _dot(q, c_state[hh].astype(BF16))
            c_state[hh] = decay_prev * c_state[hh] + incr
            hout = tot[:, :ML_DV] / jnp.maximum(jnp.abs(tot[:, ML_DV:]), floor)
            y_heads.append((mo.astype(F32) * _head_norm(hout, g_ref[:, cols])).astype(BF16))
        ys.append(jnp.concatenate(y_heads, axis=1))
    for c, y in enumerate(ys):
        y_ref[c * ML_CHUNK:(c + 1) * ML_CHUNK, :] = y
    for hh in range(nh):
        c_ref[hh] = c_state[hh]


def _mixers_kernel(q_ref, k_ref, v_ref, lf_ref, gs_ref, mq_ref, mk_ref, mv_ref, mo_ref, mif_ref,
                   hg_g_ref, tri_h_ref, sel_ref, fb_ref, ml_g_ref, tri_m_ref, yh_ref, ym_ref, st_ref, c_ref, m_ref,
                   a_scr, k_scr):
    @pl.when(pl.program_id(1) == 0)
    def _():
        st_ref[...] = jnp.zeros_like(st_ref)
        c_ref[...] = jnp.zeros_like(c_ref)
        m_ref[...] = jnp.zeros_like(m_ref)

    _hgrn2_body(q_ref, k_ref, v_ref, lf_ref, gs_ref, hg_g_ref, tri_h_ref, sel_ref, yh_ref, st_ref, a_scr, k_scr)
    _mlstm_body(mq_ref, mk_ref, mv_ref, mo_ref, mif_ref, fb_ref, ml_g_ref, tri_m_ref, ym_ref, c_ref, m_ref)


def _mixers(q, k, v, lf, gs, mq, mk, mv, mo, mif, hg_g, fb, ml_g, batch, seq, ts):
    t = q.shape[0]
    per_b = seq // ts
    tri_h = jnp.tril(jnp.ones((HG_CHUNK, HG_CHUNK), F32)).astype(BF16)
    r = jnp.arange(HG_SUB * HG_D)[:, None] // HG_D
    cidx = jnp.arange(LANES)[None, :]
    sel = ((cidx % HG_SUB == r) & (cidx < HG_CHUNK)).astype(BF16)
    tri_m = jnp.tril(jnp.ones((ML_CHUNK, ML_CHUNK), F32)).astype(BF16)
    row = lambda w: pl.BlockSpec((ts, w), lambda b, s: (b * per_b + s, 0))
    consts = [hg_g, tri_h, sel, fb, ml_g, tri_m]
    out = jax.ShapeDtypeStruct((t, HG_W), BF16)
    return pl.pallas_call(
        _mixers_kernel,
        grid=(batch, per_b),
        in_specs=[row(HG_W)] * 5 + [row(ML_W)] * 4 + [row(LANES)] + [_const_spec(a.shape) for a in consts],
        out_specs=[row(HG_W), row(ML_W)],
        out_shape=[out, out],
        scratch_shapes=[pltpu.VMEM((HG_HEADS, HG_D, HG_D), F32),
                        pltpu.VMEM((ML_HEADS, ML_DV, 2 * ML_DV), F32), pltpu.VMEM((1, LANES), F32),
                        pltpu.VMEM((ts, HG_W), F32), pltpu.VMEM((ts, HG_W), F32)],
        compiler_params=_cparams(("arbitrary", "arbitrary")),
        name="mixers",
    )(q, k, v, lf, gs, mq, mk, mv, mo, mif, *consts)


def _store_token_tiles(ref, base, val):
    n, d = val.shape
    nj = d // LANES
    for j in range(nj):
        ref[pl.ds(base + j, n, stride=nj), :] = val[:, j * LANES:(j + 1) * LANES]


def _load_token_tiles(ref, base, n, d):
    nj = d // LANES
    return [ref[pl.ds(base + j, n, stride=nj), :] for j in range(nj)]


def _merge_kernel(yh_ref, ym_ref, ga_ref, gb_ref, x_ref, modm_ref, modf_ref, g_ref, wbh_ref, wbm_ref, wo_ref,
                  *rest, routed):
    d = x_ref.shape[-1]
    merged = (ga_ref[...].astype(F32) * _dot(yh_ref[...], wbh_ref[...])
              + gb_ref[...].astype(F32) * _dot(ym_ref[...], wbm_ref[...]))
    mix = _dot(merged.astype(BF16), wo_ref[...])
    x_new = x_ref[...] + modm_ref[:, 2 * d:] * mix
    modf = modf_ref[...]
    h = _norm_modulate(x_new, g_ref[...], modf[:, :d], modf[:, d:2 * d])
    if not routed:
        xo_ref, h_ref = rest
        xo_ref[...] = x_new
        h_ref[...] = h.astype(BF16)
        return
    rw_ref, rb_ref, xo_ref, h_ref, ridx_ref, gts_ref = rest
    xo_ref[...] = x_new
    _store_token_tiles(h_ref, 0, h)
    logits = _dot(h.astype(BF16), rw_ref[...]) + rb_ref[...]
    lane = lax.broadcasted_iota(jnp.int32, logits.shape, 1)
    logits = jnp.where(lane < N_EXPERTS, logits, -jnp.inf)
    m1 = jnp.max(logits, axis=1, keepdims=True)
    i1 = jnp.min(jnp.where(logits == m1, lane, LANES), axis=1, keepdims=True)
    others = jnp.where(lane == i1, -jnp.inf, logits)
    m2 = jnp.max(others, axis=1, keepdims=True)
    i2 = jnp.min(jnp.where(others == m2, lane, LANES), axis=1, keepdims=True)
    e = jnp.exp(m2 - m1)
    g1 = 1.0 / (1.0 + e)
    ridx_ref[...] = jnp.where(lane == 0, i1, jnp.where(lane == 1, i2, 0))
    gts_ref[...] = jnp.where(lane == 0, g1, jnp.where(lane == 1, e * g1, 0.0))


def _merge(yh, ym, ga, gb, x2, modm, modf, g, wbh, wbm, wo, router, seq, tm):
    t, d = x2.shape
    per_b = seq // tm
    routed = router is not None
    row = lambda w: pl.BlockSpec((tm, w), lambda i: (i, 0))
    mod_spec = pl.BlockSpec((None, 1, modm.shape[-1]), lambda i: (i // per_b, 0, 0))
    consts = [g, wbh, wbm, wo] + (list(router) if routed else [])
    nj = d // LANES
    if routed:
        outs = [(t, tm, d, F32), (t * nj, tm * nj, LANES, F32), (t, tm, LANES, jnp.int32), (t, tm, LANES, F32)]
    else:
        outs = [(t, tm, d, F32), (t, tm, d, BF16)]
    return pl.pallas_call(
        functools.partial(_merge_kernel, routed=routed),
        grid=(t // tm,),
        in_specs=[row(HG_W), row(ML_W), row(d), row(d), row(d), mod_spec, mod_spec]
                 + [_const_spec(a.shape) for a in consts],
        out_specs=[pl.BlockSpec((br, w), lambda i: (i, 0)) for _, br, w, _ in outs],
        out_shape=[jax.ShapeDtypeStruct((n, w), dt) for n, _, w, dt in outs],
        compiler_params=_cparams(("arbitrary",)),
        name="merge_out_proj_routed" if routed else "merge_out_proj",
    )(yh, ym, ga, gb, x2, modm, modf, *consts)


def _final_norm(out, fg_ref):
    return out * lax.rsqrt(jnp.mean(out * out, axis=-1, keepdims=True) + EPS) * fg_ref[...]


def _ffn_kernel(x_ref, h_ref, mod_ref, w1_ref, w3_ref, w2_ref, fg_ref, o_ref, acc_ref, *, final_norm):
    f = pl.program_id(1)
    d = x_ref.shape[-1]

    @pl.when(f == 0)
    def _():
        acc_ref[...] = jnp.zeros_like(acc_ref)

    h = h_ref[...]
    act = (_silu(_dot(h, w1_ref[...])) * _dot(h, w3_ref[...])).astype(BF16)
    acc_ref[...] += _dot(act, w2_ref[...])

    @pl.when(f == pl.num_programs(1) - 1)
    def _():
        out = x_ref[...] + mod_ref[:, 2 * d:] * acc_ref[...]
        o_ref[...] = _final_norm(out, fg_ref) if final_norm else out


def _ffn(x2, h, mod, w1, w3, w2, fg, seq, tm, tf, final_norm):
    t, d = x2.shape
    dff = w1.shape[-1]
    per_b = seq // tm
    row = lambda w: pl.BlockSpec((tm, w), lambda i, f: (i, 0))
    return pl.pallas_call(
        functools.partial(_ffn_kernel, final_norm=final_norm),
        grid=(t // tm, dff // tf),
        in_specs=[row(d), row(d),
                  pl.BlockSpec((None, 1, mod.shape[-1]), lambda i, f: (i // per_b, 0, 0)),
                  pl.BlockSpec((d, tf), lambda i, f: (0, f)),
                  pl.BlockSpec((d, tf), lambda i, f: (0, f)),
                  pl.BlockSpec((tf, d), lambda i, f: (f, 0)),
                  pl.BlockSpec(fg.shape, lambda i, f: (0, 0))],
        out_specs=row(d),
        out_shape=jax.ShapeDtypeStruct((t, d), F32),
        scratch_shapes=[pltpu.VMEM((tm, d), F32)],
        compiler_params=_cparams(("arbitrary", "arbitrary")),
        name="swiglu_dense",
    )(x2, h, mod, w1, w3, w2, fg)


def _route_plan(ridx, t, tm):
    n_assign = 2 * t
    n_tiles = n_assign // tm + N_EXPERTS
    e_flat = jnp.concatenate([ridx[:, 0], ridx[:, 1]])
    order = jnp.sort(e_flat * n_assign + jnp.arange(n_assign, dtype=jnp.int32)) % n_assign
    counts = jnp.sum((e_flat[:, None] == jnp.arange(N_EXPERTS, dtype=jnp.int32)[None, :]).astype(jnp.int32), axis=0)
    tiles_e = (counts + tm - 1) // tm
    tile_end = jnp.cumsum(tiles_e)
    n_used = tile_end[-1]
    seg_start = jnp.cumsum(counts) - counts
    ti = jnp.arange(n_tiles, dtype=jnp.int32)
    te = jnp.minimum(jnp.sum((ti[:, None] >= tile_end[None, :]).astype(jnp.int32), axis=1), N_EXPERTS - 1)
    j = ti - (tile_end - tiles_e)[te]
    n_valid = jnp.where(ti < n_used, jnp.clip(counts[te] - j * tm, 0, tm), 0)
    r = jnp.arange(tm, dtype=jnp.int32)[None, :]
    valid = r < n_valid[:, None]
    a = order[jnp.clip(seg_start[te][:, None] + j[:, None] * tm + r, 0, n_assign - 1)]
    gather_tok = jnp.where(valid, a % t, 0)
    scatter_row = jnp.where(valid, a, n_assign + (ti % 2)[:, None] * tm + r)
    scatter_row = jnp.concatenate([n_assign + tm + r, scatter_row], axis=0)
    tile_expert = jnp.where(ti < n_used, te, te[jnp.maximum(n_used - 1, 0)])
    return (gather_tok.reshape(n_tiles, 1, tm), scatter_row.reshape(n_tiles + 1, 1, tm),
            tile_expert.astype(jnp.int32), n_used.reshape(1).astype(jnp.int32))


def _moe_kernel(te_ref, nu_ref, gcur_ref, gnext_ref, sprev_ref, scur_ref, h_hbm, w1_ref, w3_ref, w2_ref, y_hbm,
                xbuf, ybuf, gsem, ssem):
    i = pl.program_id(0)
    n_used = nu_ref[0]
    d = w1_ref.shape[0]
    nj = d // LANES
    tm = xbuf.shape[0] // (2 * nj)
    slot = i % 2
    other = 1 - slot

    def rows(ref, r, n=1):
        return ref.at[pl.ds(pl.multiple_of(r * nj, nj), n * nj)]

    def gather_row(idx_ref, sl, r):
        return pltpu.make_async_copy(rows(h_hbm, idx_ref[0, r]), rows(xbuf, sl * tm + r), gsem.at[sl])

    def scatter_row(idx_ref, sl, r):
        return pltpu.make_async_copy(rows(ybuf, sl * tm + r), rows(y_hbm, idx_ref[0, r]), ssem.at[sl])

    def wait_slot(buf, sem, sl):
        whole = rows(buf, sl * tm, tm)
        pltpu.make_async_copy(whole, whole, sem.at[sl]).wait()

    def issue_rows(*copies):
        def body(rb, carry):
            for k in range(ISSUE_UNROLL):
                for make in copies:
                    make(rb * ISSUE_UNROLL + k).start()
            return carry
        lax.fori_loop(0, tm // ISSUE_UNROLL, body, 0)

    @pl.when(i == 0)
    def _():
        issue_rows(lambda r: gather_row(gcur_ref, 0, r))
        ybuf[pl.ds(tm * nj, tm * nj), :] = jnp.zeros((tm * nj, LANES), F32)
        for half in range(2):
            fill = pltpu.make_async_copy(rows(ybuf, tm, tm), rows(y_hbm, y_hbm.shape[0] // nj - (2 - half) * tm, tm),
                                         ssem.at[1])
            fill.start()
            fill.wait()

    @pl.when(i < n_used)
    def _():
        @pl.when(i >= 1)
        def _():
            wait_slot(ybuf, ssem, slot)

        wait_slot(xbuf, gsem, slot)
        issue_rows(lambda r: gather_row(gnext_ref, other, r), lambda r: scatter_row(sprev_ref, other, r))

        base = pl.multiple_of(slot * (tm * nj), tm * nj)
        x = jnp.concatenate(_load_token_tiles(xbuf, base, tm, d), axis=1).astype(BF16)
        act = (_silu(_dot(x, w1_ref[...])) * _dot(x, w3_ref[...])).astype(BF16)
        _store_token_tiles(ybuf, base, _dot(act, w2_ref[...]))

        @pl.when(i == n_used - 1)
        def _():
            issue_rows(lambda r: scatter_row(scur_ref, slot, r))
            wait_slot(ybuf, ssem, other)
            wait_slot(ybuf, ssem, slot)
            wait_slot(xbuf, gsem, other)


def _moe(h_tiles, plan, w1, w3, w2, tm):
    gather_tok, scatter_row, tile_expert, n_used = plan
    n_tiles = gather_tok.shape[0]
    _, d, dff = w1.shape
    nj = d // LANES
    t = h_tiles.shape[0] // nj
    idx_spec = lambda off, last: pl.BlockSpec(
        (None, 1, tm), lambda i, te, nu: (jnp.minimum(i + off, last), 0, 0), memory_space=pltpu.SMEM)
    buf = pltpu.VMEM((2 * tm * nj, LANES), F32)
    grid_spec = pltpu.PrefetchScalarGridSpec(
        num_scalar_prefetch=2,
        grid=(n_tiles,),
        in_specs=[idx_spec(0, n_tiles - 1), idx_spec(1, n_tiles - 1), idx_spec(0, n_tiles), idx_spec(1, n_tiles),
                  pl.BlockSpec(memory_space=pl.ANY),
                  pl.BlockSpec((None, d, dff), lambda i, te, nu: (te[i], 0, 0)),
                  pl.BlockSpec((None, d, dff), lambda i, te, nu: (te[i], 0, 0)),
                  pl.BlockSpec((None, dff, d), lambda i, te, nu: (te[i], 0, 0))],
        out_specs=pl.BlockSpec(memory_space=pl.ANY),
        scratch_shapes=[buf, buf, pltpu.SemaphoreType.DMA((2,)), pltpu.SemaphoreType.DMA((2,))],
    )
    return pl.pallas_call(
        _moe_kernel,
        grid_spec=grid_spec,
        out_shape=jax.ShapeDtypeStruct(((2 * t + 2 * tm) * nj, LANES), F32),
        compiler_params=_cparams(("arbitrary",)),
        name="moe_experts",
    )(tile_expert, n_used, gather_tok, gather_tok, scatter_row, scatter_row, h_tiles, w1, w3, w2)


def _combine_kernel(x_ref, y0_ref, y1_ref, gts_ref, mod_ref, fg_ref, o_ref, *, final_norm):
    tm, d = x_ref.shape
    gts = gts_ref[...]
    ff = jnp.concatenate(
        [gts[:, 0:1] * a + gts[:, 1:2] * b
         for a, b in zip(_load_token_tiles(y0_ref, 0, tm, d), _load_token_tiles(y1_ref, 0, tm, d))], axis=1)
    out = x_ref[...] + mod_ref[:, 2 * d:] * ff
    o_ref[...] = _final_norm(out, fg_ref) if final_norm else out


def _combine(x2, y_tiles, gts, mod, fg, seq, tm, final_norm):
    t, d = x2.shape
    nj = d // LANES
    per_b = seq // tm
    nb = t // tm
    row = lambda w: pl.BlockSpec((tm, w), lambda i: (i, 0))
    return pl.pallas_call(
        functools.partial(_combine_kernel, final_norm=final_norm),
        grid=(nb,),
        in_specs=[row(d), pl.BlockSpec((tm * nj, LANES), lambda i: (i, 0)),
                  pl.BlockSpec((tm * nj, LANES), lambda i: (nb + i, 0)), row(LANES),
                  pl.BlockSpec((None, 1, mod.shape[-1]), lambda i: (i // per_b, 0, 0)),
                  pl.BlockSpec(fg.shape, lambda i: (0, 0))],
        out_specs=row(d),
        out_shape=jax.ShapeDtypeStruct((t, d), F32),
        compiler_params=_cparams(("arbitrary",)),
        name="moe_combine",
    )(x2, y_tiles, y_tiles, gts, mod, fg)


def _pick(n, pref):
    while n % pref:
        pref //= 2
    return pref


@jax.jit
def kernel(x, c, norm_mix_g, ada_mix_w, ada_mix_b, w_in, hg_lb_logits, hg_norm_g, ml_conv_w, ml_conv_b, ml_wq,
           ml_wk, ml_fbias, ml_norm_g, w_br_hg, w_br_ml, w_out, norm_ffn_g, ada_ffn_w, ada_ffn_b, ffn_w1,
           ffn_w3, ffn_w2, moe_router_w, moe_router_b, moe_w1, moe_w3, moe_w2, final_norm_g):
    batch, seq, d = x.shape
    depth = w_in.shape[0]
    t = batch * seq
    tm = _pick(seq, 512)
    ts_mix = _pick(seq, ML_GROUP * ML_CHUNK)

    mod_mix, mod_ffn = _ada_params(c, ada_mix_w, ada_mix_b, ada_ffn_w, ada_ffn_b)
    mod_mix = mod_mix.reshape(depth, batch, 1, 3 * d)
    mod_ffn = mod_ffn.reshape(depth, batch, 1, 3 * d)

    lb_all = jnp.cumsum(jax.nn.softmax(hg_lb_logits.astype(F32), axis=0), axis=0)
    lb_all = lb_all - lb_all[0:1]

    n_main = 4 * HG_W + 3 * ML_W
    x2 = x.reshape(t, d)
    for l in range(depth):
        wl = w_in[l]
        wm = wl[:, :n_main].astype(BF16)
        wif = jnp.pad(wl[:, n_main:n_main + 2 * ML_HEADS], ((0, 0), (0, LANES - 2 * ML_HEADS))).astype(BF16)
        wg = wl[:, n_main + 2 * ML_HEADS:].astype(BF16)
        lb = lb_all[l][None, :]
        lbp = jnp.concatenate([jnp.log(lb), jnp.log1p(-lb), 1.0 - lb, jnp.zeros((5, HG_W), F32)], axis=0)
        pad_qk = lambda w: jnp.pad(w, ((0, 0), (0, 0), (0, ML_DV - ML_DQK))).astype(BF16)
        q, k, v, gs, lf, mq, mk, mv, mo, ga, gb, mif = _proj(
            x2, mod_mix[l], norm_mix_g[l][None, :], wm, wg, wif, lbp, ml_conv_w[l], ml_conv_b[l][None, :],
            pad_qk(ml_wq[l] * (ML_DQK ** -0.5)), pad_qk(ml_wk[l]), seq, tm)

        fb = jnp.zeros((1, LANES), F32).at[0, ML_HEADS:2 * ML_HEADS].set(ml_fbias[l])
        yh, ym = _mixers(q, k, v, lf, gs, mq, mk, mv, mo, mif, hg_norm_g[l][None, :], fb,
                         ml_norm_g[l][None, :], batch, seq, ts_mix)

        j = l // 2
        routed = l % 2 == 1
        router = None
        if routed:
            router = (jnp.pad(moe_router_w[j], ((0, 0), (0, LANES - N_EXPERTS))).astype(BF16),
                      jnp.pad(moe_router_b[j], (0, LANES - N_EXPERTS))[None, :])
        merged = _merge(yh, ym, ga, gb, x2, mod_mix[l], mod_ffn[l], norm_ffn_g[l][None, :],
                        w_br_hg[l].astype(BF16), w_br_ml[l].astype(BF16), w_out[l].astype(BF16),
                        router, seq, tm)

        final = l == depth - 1
        fg = final_norm_g[None, :]
        if routed:
            x2, h2, ridx, gts = merged
            y = _moe(h2, _route_plan(ridx, t, tm), moe_w1[j].astype(BF16), moe_w3[j].astype(BF16),
                     moe_w2[j].astype(BF16), tm)
            x2 = _combine(x2, y, gts, mod_ffn[l], fg, seq, tm, final)
        else:
            x2, h2 = merged
            x2 = _ffn(x2, h2, mod_ffn[l], ffn_w1[j].astype(BF16), ffn_w3[j].astype(BF16),
                      ffn_w2[j].astype(BF16), fg, seq, tm, FF_TILE, final)
    return x2.reshape(batch, seq, d)
```

```python
import functools

import jax
import jax.numpy as jnp
from jax import lax
from jax.experimental import pallas as pl
from jax.experimental.pallas import tpu as pltpu

F32 = jnp.float32
BF16 = jnp.bfloat16

EPS = 1e-6
LANES = 128
ISSUE_UNROLL = 8
PROJ_AHEAD = 1
HG_HEADS = 4
HG_D = 128
HG_W = HG_HEADS * HG_D
ML_HEADS = 4
ML_DV = 128
ML_DQK = 64
ML_W = ML_HEADS * ML_DV
CONV_K = 4
N_EXPERTS = 8
HG_CHUNK = 64
HG_SUB = 8
HG_GROUP = 16
ML_CHUNK = 256
ML_GROUP = 4
LOG2E = 1.4426950408889634
FF_TILE = 1408
VMEM_LIMIT = 56 * 1024 * 1024


def _cparams(sem):
    return pltpu.CompilerParams(dimension_semantics=sem, vmem_limit_bytes=VMEM_LIMIT)


def _const_spec(shape):
    nd = len(shape)
    return pl.BlockSpec(shape, lambda *_: (0,) * nd, pipeline_mode=pl.Buffered(1))


def _aligned(start, multiple):
    return start if isinstance(start, int) else pl.multiple_of(start, multiple)


def _sigmoid(v):
    return 0.5 * jnp.tanh(0.5 * v) + 0.5


def _silu(v):
    return v * _sigmoid(v)


def _log_sigmoid(z):
    return jnp.minimum(z, 0.0) - jnp.log1p(jnp.exp(-jnp.abs(z)))


def _split_bf16(v):
    hi = v.astype(BF16)
    lo = (v - hi.astype(F32)).astype(BF16)
    return hi, lo


def _dot(a, b):
    return jnp.dot(a, b, preferred_element_type=F32)


def _dot_nt(a, b):
    return lax.dot_general(a, b, (((1,), (1,)), ((), ())), preferred_element_type=F32)


def _dot_tn(a, b):
    return lax.dot_general(a, b, (((0,), (0,)), ((), ())), preferred_element_type=F32)


def _ada_kernel(c_ref, wm_ref, bm_ref, wf_ref, bf_ref, om_ref, of_ref):
    c = c_ref[...]
    ca = _silu(c).astype(BF16)
    om_ref[...] = _dot(ca, wm_ref[...].astype(BF16)) + bm_ref[...]
    of_ref[...] = _dot(ca, wf_ref[...].astype(BF16)) + bf_ref[...]


def _ada_params(c, wm, bm, wf, bf):
    depth, d, d3 = wm.shape
    b = c.shape[0]
    nt = d3 // d
    w_spec = pl.BlockSpec((None, d, d), lambda l, j: (l, 0, j))
    b_spec = pl.BlockSpec((None, 1, d), lambda l, j: (l, 0, j))
    o_spec = pl.BlockSpec((None, b, d), lambda l, j: (l, 0, j))
    out = jax.ShapeDtypeStruct((depth, b, d3), F32)
    return pl.pallas_call(
        _ada_kernel,
        grid=(depth, nt),
        in_specs=[pl.BlockSpec((b, d), lambda l, j: (0, 0)), w_spec, b_spec, w_spec, b_spec],
        out_specs=[o_spec, o_spec],
        out_shape=[out, out],
        compiler_params=_cparams(("arbitrary", "arbitrary")),
        name="ada_params",
    )(c, wm, bm.reshape(depth, 1, d3), wf, bf.reshape(depth, 1, d3))


def _norm_modulate(x, g, shift, scale):
    ms = jnp.mean(x * x, axis=-1, keepdims=True)
    y = x * lax.rsqrt(ms + EPS) * g
    return y * (1.0 + scale) + shift


def _proj_kernel(x_ref, mod_ref, g_ref, wm_ref, wg_ref, wif_ref, lbp_ref, cw_ref, cb_ref, wq_ref, wk_ref,
                 q_ref, k_ref, v_ref, gs_ref, lf_ref, mq_ref, mk_ref, mv_ref, mo_ref, ga_ref, gb_ref, mif_ref,
                 carry_ref, *, per_b):
    tm, d = x_ref.shape
    mod = mod_ref[...]
    h = _norm_modulate(x_ref[...], g_ref[...], mod[:, :d], mod[:, d:2 * d]).astype(BF16)

    def piece(w_ref, j, width=HG_W):
        return _dot(h, w_ref[:, j * width:(j + 1) * width])

    @pl.when(pl.program_id(0) % per_b == 0)
    def _():
        carry_ref[...] = jnp.zeros_like(carry_ref)

    log_lb, log1m_lb, one_m_lb = lbp_ref[0:1, :], lbp_ref[1:2, :], lbp_ref[2:3, :]

    def forget_gate(zf):
        t = log1m_lb + _log_sigmoid(zf)
        lf_ref[...] = jnp.maximum(log_lb, t) + jnp.log1p(jnp.exp(-jnp.abs(log_lb - t)))
        k_ref[...] = (one_m_lb * _sigmoid(-zf)).astype(BF16)

    def mlstm_qk(u):
        ext = jnp.concatenate([carry_ref[...], u], axis=0)
        carry_ref[...] = u[tm - 8:, :]
        conv = cb_ref[...]
        for j in range(CONV_K):
            off = 8 - (CONV_K - 1) + j
            conv = conv + cw_ref[j:j + 1, :] * ext[off:off + tm, :]
        uc = _silu(conv).astype(BF16)
        for hh in range(ML_HEADS):
            cols = slice(hh * ML_DV, (hh + 1) * ML_DV)
            mq_ref[:, cols] = _dot(uc[:, cols], wq_ref[hh]).astype(BF16)
            mk_ref[:, cols] = _dot(uc[:, cols], wk_ref[hh]).astype(BF16)

    def store(ref, fn=None):
        def epilogue(val):
            ref[...] = (val if fn is None else fn(val)).astype(ref.dtype)
        return epilogue

    stages = [
        (lambda: piece(wm_ref, 1), forget_gate),
        (lambda: piece(wg_ref, 0, d), store(ga_ref, _sigmoid)),
        (lambda: piece(wm_ref, 4), mlstm_qk),
        (lambda: piece(wg_ref, 1, d), store(gb_ref, _sigmoid)),
        (lambda: piece(wm_ref, 0), store(q_ref, _silu)),
        (lambda: piece(wm_ref, 2), store(v_ref)),
        (lambda: piece(wm_ref, 3), store(gs_ref, _silu)),
        (lambda: piece(wm_ref, 5), store(mv_ref)),
        (lambda: piece(wm_ref, 6), store(mo_ref, _sigmoid)),
        (lambda: _dot(h, wif_ref[...]), store(mif_ref)),
    ]
    pending = []
    for project, epilogue in stages:
        pending.append((project(), epilogue))
        if len(pending) > PROJ_AHEAD:
            val, done = pending.pop(0)
            done(val)
    for val, done in pending:
        done(val)


def _proj(x2, mod, g, wm, wg, wif, lbp, cw, cb, wq, wk, seq, tm):
    t, d = x2.shape
    per_b = seq // tm
    row = lambda w: pl.BlockSpec((tm, w), lambda i: (i, 0))
    sds = lambda w, dt: jax.ShapeDtypeStruct((t, w), dt)
    outs = [(HG_W, BF16)] * 4 + [(HG_W, F32)] + [(ML_W, BF16)] * 4 + [(d, BF16)] * 2 + [(LANES, F32)]
    consts = [g, wm, wg, wif, lbp, cw, cb, wq, wk]
    return pl.pallas_call(
        functools.partial(_proj_kernel, per_b=per_b),
        grid=(t // tm,),
        in_specs=[row(d), pl.BlockSpec((None, 1, mod.shape[-1]), lambda i: (i // per_b, 0, 0))]
                 + [_const_spec(a.shape) for a in consts],
        out_specs=[row(w) for w, _ in outs],
        out_shape=[sds(w, dt) for w, dt in outs],
        scratch_shapes=[pltpu.VMEM((8, ML_W), F32)],
        compiler_params=_cparams(("arbitrary",)),
        name="in_proj",
    )(x2, mod, g, *consts[1:])


def _head_norm(o, g):
    return o * lax.rsqrt(jnp.mean(o * o, axis=-1, keepdims=True) + EPS) * g


def _hgrn2_body(q_ref, k_ref, v_ref, lf_ref, gs_ref, g_ref, tri_ref, sel_ref, y_ref, st_ref, a_scr, k_scr):
    ts = q_ref.shape[0]
    nsub = HG_CHUNK // HG_SUB

    tri = tri_ref[...]
    row = lax.broadcasted_iota(jnp.int32, (HG_CHUNK, HG_CHUNK), 0)
    col = lax.broadcasted_iota(jnp.int32, (HG_CHUNK, HG_CHUNK), 1)
    diag_mask = (row // HG_SUB == col // HG_SUB) & (col <= row)

    def intra_pairs(c):
        rows = pl.ds(_aligned(c * HG_CHUNK, HG_CHUNK), HG_CHUNK)
        hi, lo = _split_bf16(lf_ref[rows, :])
        a_all = (_dot(tri, hi) + _dot(tri, lo)) * LOG2E
        heads = []
        for hh in range(HG_HEADS):
            cols = slice(hh * HG_D, (hh + 1) * HG_D)
            heads.append((a_all[:, cols], q_ref[rows, cols].astype(F32), k_ref[rows, cols].astype(F32),
                          v_ref[rows, cols], gs_ref[rows, cols]))
        a_scr[rows, :] = a_all
        k_scr[rows, :] = k_ref[rows, :].astype(F32)
        p_heads = []
        for hh, (a, qf, _, _, _) in enumerate(heads):
            cols = slice(hh * HG_D, (hh + 1) * HG_D)
            pieces = [[] for _ in range(HG_SUB)]
            for blk in range(nsub):
                a_b = a[blk * HG_SUB:(blk + 1) * HG_SUB]
                q_b = qf[blk * HG_SUB:(blk + 1) * HG_SUB]
                for s in range(HG_SUB):
                    key = pl.ds(c * HG_CHUNK + blk * HG_SUB + s, 1)
                    dec = jnp.exp2(jnp.minimum(a_b - a_scr[key, cols], 0.0))
                    pieces[s].append(q_b * k_scr[key, cols] * dec)
            p_heads.append(jnp.concatenate([jnp.concatenate(p, axis=0).astype(BF16) for p in pieces], axis=1))
        return heads, _dot(jnp.concatenate(p_heads, axis=0), sel_ref[...])

    def intra_rest(heads, diag):
        out = []
        for hh, (a, qf, kf, vb, gs) in enumerate(heads):
            a_last = a[HG_CHUNK - 1:HG_CHUNK, :]
            qs, ks = [], []
            for j in range(nsub - 1):
                e = a[(j + 1) * HG_SUB - 1:(j + 1) * HG_SUB, :]
                lo_r, hi_r = j * HG_SUB, (j + 1) * HG_SUB
                kt = kf[lo_r:hi_r] * jnp.exp2(e - a[lo_r:hi_r])
                qt = qf[hi_r:] * jnp.exp2(a[hi_r:] - e)
                ks.append(jnp.concatenate(
                    [jnp.zeros((lo_r, HG_D), F32), kt, jnp.zeros((HG_CHUNK - hi_r, HG_D), F32)], axis=0)
                    if lo_r else jnp.concatenate([kt, jnp.zeros((HG_CHUNK - hi_r, HG_D), F32)], axis=0))
                qs.append(jnp.concatenate([jnp.zeros((hi_r, HG_D), F32), qt], axis=0))
            s_off = _dot_nt(jnp.concatenate(qs, axis=1).astype(BF16), jnp.concatenate(ks, axis=1).astype(BF16))
            s_diag = diag[hh * HG_CHUNK:(hh + 1) * HG_CHUNK, :HG_CHUNK]
            s_full = s_off + jnp.where(diag_mask, s_diag, 0.0)
            k_dec = (kf * jnp.exp2(a_last - a)).astype(BF16)
            out.append((_dot(s_full.astype(BF16), vb), (qf * jnp.exp2(a)).astype(BF16), jnp.exp2(a_last),
                        _dot_tn(vb, k_dec), gs))
        return out

    def group(gi, states):
        parts, ahead = [], None
        for c in range(HG_GROUP):
            nxt = intra_pairs(gi * HG_GROUP + c)
            if ahead is not None:
                parts.append(intra_rest(*ahead))
            ahead = nxt
        parts.append(intra_rest(*ahead))
        states = list(states)
        ys = []
        for part in parts:
            y_heads = []
            for hh, (o_intra, q_dec, decay, incr, gs) in enumerate(part):
                st = states[hh]
                o = o_intra + _dot_nt(q_dec, st.astype(BF16))
                states[hh] = st * decay + incr
                cols = slice(hh * HG_D, (hh + 1) * HG_D)
                y_heads.append((_head_norm(o, g_ref[:, cols]) * gs.astype(F32)).astype(BF16))
            ys.append(jnp.concatenate(y_heads, axis=1))
        for c, y in enumerate(ys):
            y_ref[pl.ds(_aligned((gi * HG_GROUP + c) * HG_CHUNK, HG_CHUNK), HG_CHUNK), :] = y
        return tuple(states)

    states = tuple(st_ref[hh] for hh in range(HG_HEADS))
    n_groups = ts // (HG_CHUNK * HG_GROUP)
    states = group(0, states) if n_groups == 1 else lax.fori_loop(0, n_groups, group, states)
    for hh in range(HG_HEADS):
        st_ref[hh] = states[hh]


def _mlstm_body(mq_ref, mk_ref, mv_ref, mo_ref, mif_ref, fb_ref, g_ref, tri_ref, y_ref, c_ref, m_ref):
    ts = mq_ref.shape[0]
    nh = ML_HEADS
    n_chunks = ts // ML_CHUNK

    tri = tri_ref[...]
    row = lax.broadcasted_iota(jnp.int32, (ML_CHUNK, ML_CHUNK), 0)
    col = lax.broadcasted_iota(jnp.int32, (ML_CHUNK, ML_CHUNK), 1)
    causal = col <= row
    ones_blk = jnp.ones((ML_CHUNK, ML_DV), BF16)
    lane = lax.broadcasted_iota(jnp.int32, (ML_CHUNK, LANES), 1)
    lane8 = lax.broadcasted_iota(jnp.int32, (2 * nh, ML_CHUNK), 1)

    m_prev = m_ref[...]
    gates = []
    for c in range(n_chunks):
        rows = slice(c * ML_CHUNK, (c + 1) * ML_CHUNK)
        mif = mif_ref[rows, :]
        hi, lo = _split_bf16(_log_sigmoid(mif + fb_ref[...]))
        bcum = _dot(tri, hi) + _dot(tri, lo)
        z = jnp.where(lane < nh, mif, bcum)
        b_last = z[ML_CHUNK - 1:ML_CHUNK, :]
        state_log = b_last - z + pltpu.roll(z, nh, axis=1)
        m_new = jnp.maximum(b_last + m_prev, jnp.max(state_log, axis=0, keepdims=True))
        gates.append((z, z + m_prev, jnp.exp(state_log - m_new), jnp.exp(b_last + m_prev - m_new)))
        m_prev = m_new
    m_ref[...] = m_prev

    parts = []
    for c, (z, inter_log, w_s, decay_prev) in enumerate(gates):
        rows = slice(c * ML_CHUNK, (c + 1) * ML_CHUNK)
        zt8 = z.T[0:2 * nh, :]
        r8 = pltpu.roll(zt8, nh, axis=0) - zt8
        cm = r8
        shift = 1
        while shift < ML_CHUNK:
            cm = jnp.maximum(cm, jnp.where(lane8 >= shift, pltpu.roll(cm, shift, axis=1), -jnp.inf))
            shift *= 2
        cm_cols = jnp.concatenate([cm, jnp.zeros((LANES - 2 * nh, ML_CHUNK), F32)], axis=0).T
        m_t = jnp.maximum(z + cm_cols, inter_log)
        w_inter = jnp.exp(inter_log - m_t)
        floor = jnp.exp(-m_t)
        g2 = (z - m_t) * LOG2E
        r2 = r8 * LOG2E
        heads = []
        for hh in range(nh):
            cols = slice(hh * ML_DV, (hh + 1) * ML_DV)
            sc = slice(nh + hh, nh + hh + 1)
            q = mq_ref[rows, cols]
            k = mk_ref[rows, cols]
            v_aug = jnp.concatenate([mv_ref[rows, cols], ones_blk], axis=1)
            w = jnp.where(causal, jnp.exp2(g2[:, sc] + r2[nh + hh:nh + hh + 1, :]), 0.0)
            p = (_dot_nt(q, k) * w).astype(BF16)
            heads.append((q, _dot(p, v_aug), w_inter[:, sc], floor[:, sc], decay_prev[:, sc],
                          _dot_tn((w_s[:, sc] * k.astype(F32)).astype(BF16), v_aug), mo_ref[rows, cols]))
        parts.append(heads)

    c_state = [c_ref[hh] for hh in range(nh)]
    ys = []
    for heads in parts:
        y_heads = []
        for hh, (q, pv, w_inter, floor, decay_prev, incr, mo) in enumerate(heads):
            cols = slice(hh * ML_DV, (hh + 1) * ML_DV)
            tot = pv + w_inter * _dot(q, c_state[hh].astype(BF16))
            c_state[hh] = decay_prev * c_state[hh] + incr
            hout = tot[:, :ML_DV] / jnp.maximum(jnp.abs(tot[:, ML_DV:]), floor)
            y_heads.append((mo.astype(F32) * _head_norm(hout, g_ref[:, cols])).astype(BF16))
        ys.append(jnp.concatenate(y_heads, axis=1))
    for c, y in enumerate(ys):
        y_ref[c * ML_CHUNK:(c + 1) * ML_CHUNK, :] = y
    for hh in range(nh):
        c_ref[hh] = c_state[hh]


def _mixers_kernel(q_ref, k_ref, v_ref, lf_ref, gs_ref, mq_ref, mk_ref, mv_ref, mo_ref, mif_ref,
                   hg_g_ref, tri_h_ref, sel_ref, fb_ref, ml_g_ref, tri_m_ref, yh_ref, ym_ref, st_ref, c_ref, m_ref,
                   a_scr, k_scr):
    @pl.when(pl.program_id(1) == 0)
    def _():
        st_ref[...] = jnp.zeros_like(st_ref)
        c_ref[...] = jnp.zeros_like(c_ref)
        m_ref[...] = jnp.zeros_like(m_ref)

    _hgrn2_body(q_ref, k_ref, v_ref, lf_ref, gs_ref, hg_g_ref, tri_h_ref, sel_ref, yh_ref, st_ref, a_scr, k_scr)
    _mlstm_body(mq_ref, mk_ref, mv_ref, mo_ref, mif_ref, fb_ref, ml_g_ref, tri_m_ref, ym_ref, c_ref, m_ref)


def _mixers(q, k, v, lf, gs, mq, mk, mv, mo, mif, hg_g, fb, ml_g, batch, seq, ts):
    t = q.shape[0]
    per_b = seq // ts
    tri_h = jnp.tril(jnp.ones((HG_CHUNK, HG_CHUNK), F32)).astype(BF16)
    r = jnp.arange(HG_SUB * HG_D)[:, None] // HG_D
    cidx = jnp.arange(LANES)[None, :]
    sel = ((cidx % HG_SUB == r) & (cidx < HG_CHUNK)).astype(BF16)
    tri_m = jnp.tril(jnp.ones((ML_CHUNK, ML_CHUNK), F32)).astype(BF16)
    row = lambda w: pl.BlockSpec((ts, w), lambda b, s: (b * per_b + s, 0))
    consts = [hg_g, tri_h, sel, fb, ml_g, tri_m]
    out = jax.ShapeDtypeStruct((t, HG_W), BF16)
    return pl.pallas_call(
        _mixers_kernel,
        grid=(batch, per_b),
        in_specs=[row(HG_W)] * 5 + [row(ML_W)] * 4 + [row(LANES)] + [_const_spec(a.shape) for a in consts],
        out_specs=[row(HG_W), row(ML_W)],
        out_shape=[out, out],
        scratch_shapes=[pltpu.VMEM((HG_HEADS, HG_D, HG_D), F32),
                        pltpu.VMEM((ML_HEADS, ML_DV, 2 * ML_DV), F32), pltpu.VMEM((1, LANES), F32),
                        pltpu.VMEM((ts, HG_W), F32), pltpu.VMEM((ts, HG_W), F32)],
        compiler_params=_cparams(("arbitrary", "arbitrary")),
        name="mixers",
    )(q, k, v, lf, gs, mq, mk, mv, mo, mif, *consts)


def _store_token_tiles(ref, base, val):
    n, d = val.shape
    nj = d // LANES
    for j in range(nj):
        ref[pl.ds(base + j, n, stride=nj), :] = val[:, j * LANES:(j + 1) * LANES]


def _load_token_tiles(ref, base, n, d):
    nj = d // LANES
    return [ref[pl.ds(base + j, n, stride=nj), :] for j in range(nj)]


def _merge_kernel(yh_ref, ym_ref, ga_ref, gb_ref, x_ref, modm_ref, modf_ref, g_ref, wbh_ref, wbm_ref, wo_ref,
                  *rest, routed):
    d = x_ref.shape[-1]
    merged = (ga_ref[...].astype(F32) * _dot(yh_ref[...], wbh_ref[...])
              + gb_ref[...].astype(F32) * _dot(ym_ref[...], wbm_ref[...]))
    mix = _dot(merged.astype(BF16), wo_ref[...])
    x_new = x_ref[...] + modm_ref[:, 2 * d:] * mix
    modf = modf_ref[...]
    h = _norm_modulate(x_new, g_ref[...], modf[:, :d], modf[:, d:2 * d])
    if not routed:
        xo_ref, h_ref = rest
        xo_ref[...] = x_new
        h_ref[...] = h.astype(BF16)
        return
    rw_ref, rb_ref, xo_ref, h_ref, ridx_ref, gts_ref = rest
    xo_ref[...] = x_new
    _store_token_tiles(h_ref, 0, h)
    logits = _dot(h.astype(BF16), rw_ref[...]) + rb_ref[...]
    lane = lax.broadcasted_iota(jnp.int32, logits.shape, 1)
    logits = jnp.where(lane < N_EXPERTS, logits, -jnp.inf)
    m1 = jnp.max(logits, axis=1, keepdims=True)
    i1 = jnp.min(jnp.where(logits == m1, lane, LANES), axis=1, keepdims=True)
    others = jnp.where(lane == i1, -jnp.inf, logits)
    m2 = jnp.max(others, axis=1, keepdims=True)
    i2 = jnp.min(jnp.where(others == m2, lane, LANES), axis=1, keepdims=True)
    e = jnp.exp(m2 - m1)
    g1 = 1.0 / (1.0 + e)
    ridx_ref[...] = jnp.where(lane == 0, i1, jnp.where(lane == 1, i2, 0))
    gts_ref[...] = jnp.where(lane == 0, g1, jnp.where(lane == 1, e * g1, 0.0))


def _merge(yh, ym, ga, gb, x2, modm, modf, g, wbh, wbm, wo, router, seq, tm):
    t, d = x2.shape
    per_b = seq // tm
    routed = router is not None
    row = lambda w: pl.BlockSpec((tm, w), lambda i: (i, 0))
    mod_spec = pl.BlockSpec((None, 1, modm.shape[-1]), lambda i: (i // per_b, 0, 0))
    consts = [g, wbh, wbm, wo] + (list(router) if routed else [])
    nj = d // LANES
    if routed:
        outs = [(t, tm, d, F32), (t * nj, tm * nj, LANES, F32), (t, tm, LANES, jnp.int32), (t, tm, LANES, F32)]
    else:
        outs = [(t, tm, d, F32), (t, tm, d, BF16)]
    return pl.pallas_call(
        functools.partial(_merge_kernel, routed=routed),
        grid=(t // tm,),
        in_specs=[row(HG_W), row(ML_W), row(d), row(d), row(d), mod_spec, mod_spec]
                 + [_const_spec(a.shape) for a in consts],
        out_specs=[pl.BlockSpec((br, w), lambda i: (i, 0)) for _, br, w, _ in outs],
        out_shape=[jax.ShapeDtypeStruct((n, w), dt) for n, _, w, dt in outs],
        compiler_params=_cparams(("arbitrary",)),
        name="merge_out_proj_routed" if routed else "merge_out_proj",
    )(yh, ym, ga, gb, x2, modm, modf, *consts)


def _final_norm(out, fg_ref):
    return out * lax.rsqrt(jnp.mean(out * out, axis=-1, keepdims=True) + EPS) * fg_ref[...]


def _ffn_kernel(x_ref, h_ref, mod_ref, w1_ref, w3_ref, w2_ref, fg_ref, o_ref, acc_ref, *, final_norm):
    f = pl.program_id(1)
    d = x_ref.shape[-1]

    @pl.when(f == 0)
    def _():
        acc_ref[...] = jnp.zeros_like(acc_ref)

    h = h_ref[...]
    act = (_silu(_dot(h, w1_ref[...])) * _dot(h, w3_ref[...])).astype(BF16)
    acc_ref[...] += _dot(act, w2_ref[...])

    @pl.when(f == pl.num_programs(1) - 1)
    def _():
        out = x_ref[...] + mod_ref[:, 2 * d:] * acc_ref[...]
        o_ref[...] = _final_norm(out, fg_ref) if final_norm else out


def _ffn(x2, h, mod, w1, w3, w2, fg, seq, tm, tf, final_norm):
    t, d = x2.shape
    dff = w1.shape[-1]
    per_b = seq // tm
    row = lambda w: pl.BlockSpec((tm, w), lambda i, f: (i, 0))
    return pl.pallas_call(
        functools.partial(_ffn_kernel, final_norm=final_norm),
        grid=(t // tm, dff // tf),
        in_specs=[row(d), row(d),
                  pl.BlockSpec((None, 1, mod.shape[-1]), lambda i, f: (i // per_b, 0, 0)),
                  pl.BlockSpec((d, tf), lambda i, f: (0, f)),
                  pl.BlockSpec((d, tf), lambda i, f: (0, f)),
                  pl.BlockSpec((tf, d), lambda i, f: (f, 0)),
                  pl.BlockSpec(fg.shape, lambda i, f: (0, 0))],
        out_specs=row(d),
        out_shape=jax.ShapeDtypeStruct((t, d), F32),
        scratch_shapes=[pltpu.VMEM((tm, d), F32)],
        compiler_params=_cparams(("arbitrary", "arbitrary")),
        name="swiglu_dense",
    )(x2, h, mod, w1, w3, w2, fg)


def _route_plan(ridx, t, tm):
    n_assign = 2 * t
    n_tiles = n_assign // tm + N_EXPERTS
    e_flat = jnp.concatenate([ridx[:, 0], ridx[:, 1]])
    order = jnp.sort(e_flat * n_assign + jnp.arange(n_assign, dtype=jnp.int32)) % n_assign
    counts = jnp.sum((e_flat[:, None] == jnp.arange(N_EXPERTS, dtype=jnp.int32)[None, :]).astype(jnp.int32), axis=0)
    tiles_e = (counts + tm - 1) // tm
    tile_end = jnp.cumsum(tiles_e)
    n_used = tile_end[-1]
    seg_start = jnp.cumsum(counts) - counts
    ti = jnp.arange(n_tiles, dtype=jnp.int32)
    te = jnp.minimum(jnp.sum((ti[:, None] >= tile_end[None, :]).astype(jnp.int32), axis=1), N_EXPERTS - 1)
    j = ti - (tile_end - tiles_e)[te]
    n_valid = jnp.where(ti < n_used, jnp.clip(counts[te] - j * tm, 0, tm), 0)
    r = jnp.arange(tm, dtype=jnp.int32)[None, :]
    valid = r < n_valid[:, None]
    a = order[jnp.clip(seg_start[te][:, None] + j[:, None] * tm + r, 0, n_assign - 1)]
    gather_tok = jnp.where(valid, a % t, 0)
    scatter_row = jnp.where(valid, a, n_assign + (ti % 2)[:, None] * tm + r)
    scatter_row = jnp.concatenate([n_assign + tm + r, scatter_row], axis=0)
    tile_expert = jnp.where(ti < n_used, te, te[jnp.maximum(n_used - 1, 0)])
    return (gather_tok.reshape(n_tiles, 1, tm), scatter_row.reshape(n_tiles + 1, 1, tm),
            tile_expert.astype(jnp.int32), n_used.reshape(1).astype(jnp.int32))


def _moe_kernel(te_ref, nu_ref, gcur_ref, gnext_ref, sprev_ref, scur_ref, h_hbm, w1_ref, w3_ref, w2_ref, y_hbm,
                xbuf, ybuf, gsem, ssem):
    i = pl.program_id(0)
    n_used = nu_ref[0]
    d = w1_ref.shape[0]
    nj = d // LANES
    tm = xbuf.shape[0] // (2 * nj)
    slot = i % 2
    other = 1 - slot

    def rows(ref, r, n=1):
        return ref.at[pl.ds(pl.multiple_of(r * nj, nj), n * nj)]

    def gather_row(idx_ref, sl, r):
        return pltpu.make_async_copy(rows(h_hbm, idx_ref[0, r]), rows(xbuf, sl * tm + r), gsem.at[sl])

    def scatter_row(idx_ref, sl, r):
        return pltpu.make_async_copy(rows(ybuf, sl * tm + r), rows(y_hbm, idx_ref[0, r]), ssem.at[sl])

    def wait_slot(buf, sem, sl):
        whole = rows(buf, sl * tm, tm)
        pltpu.make_async_copy(whole, whole, sem.at[sl]).wait()

    def issue_rows(*copies):
        def body(rb, carry):
            for k in range(ISSUE_UNROLL):
                for make in copies:
                    make(rb * ISSUE_UNROLL + k).start()
            return carry
        lax.fori_loop(0, tm // ISSUE_UNROLL, body, 0)

    @pl.when(i == 0)
    def _():
        issue_rows(lambda r: gather_row(gcur_ref, 0, r))
        ybuf[pl.ds(tm * nj, tm * nj), :] = jnp.zeros((tm * nj, LANES), F32)
        for half in range(2):
            fill = pltpu.make_async_copy(rows(ybuf, tm, tm), rows(y_hbm, y_hbm.shape[0] // nj - (2 - half) * tm, tm),
                                         ssem.at[1])
            fill.start()
            fill.wait()

    @pl.when(i < n_used)
    def _():
        @pl.when(i >= 1)
        def _():
            wait_slot(ybuf, ssem, slot)

        wait_slot(xbuf, gsem, slot)
        issue_rows(lambda r: gather_row(gnext_ref, other, r), lambda r: scatter_row(sprev_ref, other, r))

        base = pl.multiple_of(slot * (tm * nj), tm * nj)
        x = jnp.concatenate(_load_token_tiles(xbuf, base, tm, d), axis=1).astype(BF16)
        act = (_silu(_dot(x, w1_ref[...])) * _dot(x, w3_ref[...])).astype(BF16)
        _store_token_tiles(ybuf, base, _dot(act, w2_ref[...]))

        @pl.when(i == n_used - 1)
        def _():
            issue_rows(lambda r: scatter_row(scur_ref, slot, r))
            wait_slot(ybuf, ssem, other)
            wait_slot(ybuf, ssem, slot)
            wait_slot(xbuf, gsem, other)


def _moe(h_tiles, plan, w1, w3, w2, tm):
    gather_tok, scatter_row, tile_expert, n_used = plan
    n_tiles = gather_tok.shape[0]
    _, d, dff = w1.shape
    nj = d // LANES
    t = h_tiles.shape[0] // nj
    idx_spec = lambda off, last: pl.BlockSpec(
        (None, 1, tm), lambda i, te, nu: (jnp.minimum(i + off, last), 0, 0), memory_space=pltpu.SMEM)
    buf = pltpu.VMEM((2 * tm * nj, LANES), F32)
    grid_spec = pltpu.PrefetchScalarGridSpec(
        num_scalar_prefetch=2,
        grid=(n_tiles,),
        in_specs=[idx_spec(0, n_tiles - 1), idx_spec(1, n_tiles - 1), idx_spec(0, n_tiles), idx_spec(1, n_tiles),
                  pl.BlockSpec(memory_space=pl.ANY),
                  pl.BlockSpec((None, d, dff), lambda i, te, nu: (te[i], 0, 0)),
                  pl.BlockSpec((None, d, dff), lambda i, te, nu: (te[i], 0, 0)),
                  pl.BlockSpec((None, dff, d), lambda i, te, nu: (te[i], 0, 0))],
        out_specs=pl.BlockSpec(memory_space=pl.ANY),
        scratch_shapes=[buf, buf, pltpu.SemaphoreType.DMA((2,)), pltpu.SemaphoreType.DMA((2,))],
    )
    return pl.pallas_call(
        _moe_kernel,
        grid_spec=grid_spec,
        out_shape=jax.ShapeDtypeStruct(((2 * t + 2 * tm) * nj, LANES), F32),
        compiler_params=_cparams(("arbitrary",)),
        name="moe_experts",
    )(tile_expert, n_used, gather_tok, gather_tok, scatter_row, scatter_row, h_tiles, w1, w3, w2)


def _combine_kernel(x_ref, y0_ref, y1_ref, gts_ref, mod_ref, fg_ref, o_ref, *, final_norm):
    tm, d = x_ref.shape
    gts = gts_ref[...]
    ff = jnp.concatenate(
        [gts[:, 0:1] * a + gts[:, 1:2] * b
         for a, b in zip(_load_token_tiles(y0_ref, 0, tm, d), _load_token_tiles(y1_ref, 0, tm, d))], axis=1)
    out = x_ref[...] + mod_ref[:, 2 * d:] * ff
    o_ref[...] = _final_norm(out, fg_ref) if final_norm else out


def _combine(x2, y_tiles, gts, mod, fg, seq, tm, final_norm):
    t, d = x2.shape
    nj = d // LANES
    per_b = seq // tm
    nb = t // tm
    row = lambda w: pl.BlockSpec((tm, w), lambda i: (i, 0))
    return pl.pallas_call(
        functools.partial(_combine_kernel, final_norm=final_norm),
        grid=(nb,),
        in_specs=[row(d), pl.BlockSpec((tm * nj, LANES), lambda i: (i, 0)),
                  pl.BlockSpec((tm * nj, LANES), lambda i: (nb + i, 0)), row(LANES),
                  pl.BlockSpec((None, 1, mod.shape[-1]), lambda i: (i // per_b, 0, 0)),
                  pl.BlockSpec(fg.shape, lambda i: (0, 0))],
        out_specs=row(d),
        out_shape=jax.ShapeDtypeStruct((t, d), F32),
        compiler_params=_cparams(("arbitrary",)),
        name="moe_combine",
    )(x2, y_tiles, y_tiles, gts, mod, fg)


def _pick(n, pref):
    while n % pref:
        pref //= 2
    return pref


@jax.jit
def kernel(x, c, norm_mix_g, ada_mix_w, ada_mix_b, w_in, hg_lb_logits, hg_norm_g, ml_conv_w, ml_conv_b, ml_wq,
           ml_wk, ml_fbias, ml_norm_g, w_br_hg, w_br_ml, w_out, norm_ffn_g, ada_ffn_w, ada_ffn_b, ffn_w1,
           ffn_w3, ffn_w2, moe_router_w, moe_router_b, moe_w1, moe_w3, moe_w2, final_norm_g):
    batch, seq, d = x.shape
    depth = w_in.shape[0]
    t = batch * seq
    tm = _pick(seq, 512)
    ts_mix = _pick(seq, ML_GROUP * ML_CHUNK)

    mod_mix, mod_ffn = _ada_params(c, ada_mix_w, ada_mix_b, ada_ffn_w, ada_ffn_b)
    mod_mix = mod_mix.reshape(depth, batch, 1, 3 * d)
    mod_ffn = mod_ffn.reshape(depth, batch, 1, 3 * d)

    lb_all = jnp.cumsum(jax.nn.softmax(hg_lb_logits.astype(F32), axis=0), axis=0)
    lb_all = lb_all - lb_all[0:1]

    n_main = 4 * HG_W + 3 * ML_W
    x2 = x.reshape(t, d)
    for l in range(depth):
        wl = w_in[l]
        wm = wl[:, :n_main].astype(BF16)
        wif = jnp.pad(wl[:, n_main:n_main + 2 * ML_HEADS], ((0, 0), (0, LANES - 2 * ML_HEADS))).astype(BF16)
        wg = wl[:, n_main + 2 * ML_HEADS:].astype(BF16)
        lb = lb_all[l][None, :]
        lbp = jnp.concatenate([jnp.log(lb), jnp.log1p(-lb), 1.0 - lb, jnp.zeros((5, HG_W), F32)], axis=0)
        pad_qk = lambda w: jnp.pad(w, ((0, 0), (0, 0), (0, ML_DV - ML_DQK))).astype(BF16)
        q, k, v, gs, lf, mq, mk, mv, mo, ga, gb, mif = _proj(
            x2, mod_mix[l], norm_mix_g[l][None, :], wm, wg, wif, lbp, ml_conv_w[l], ml_conv_b[l][None, :],
            pad_qk(ml_wq[l] * (ML_DQK ** -0.5)), pad_qk(ml_wk[l]), seq, tm)

        fb = jnp.zeros((1, LANES), F32).at[0, ML_HEADS:2 * ML_HEADS].set(ml_fbias[l])
        yh, ym = _mixers(q, k, v, lf, gs, mq, mk, mv, mo, mif, hg_norm_g[l][None, :], fb,
                         ml_norm_g[l][None, :], batch, seq, ts_mix)

        j = l // 2
        routed = l % 2 == 1
        router = None
        if routed:
            router = (jnp.pad(moe_router_w[j], ((0, 0), (0, LANES - N_EXPERTS))).astype(BF16),
                      jnp.pad(moe_router_b[j], (0, LANES - N_EXPERTS))[None, :])
        merged = _merge(yh, ym, ga, gb, x2, mod_mix[l], mod_ffn[l], norm_ffn_g[l][None, :],
                        w_br_hg[l].astype(BF16), w_br_ml[l].astype(BF16), w_out[l].astype(BF16),
                        router, seq, tm)

        final = l == depth - 1
        fg = final_norm_g[None, :]
        if routed:
            x2, h2, ridx, gts = merged
            y = _moe(h2, _route_plan(ridx, t, tm), moe_w1[j].astype(BF16), moe_w3[j].astype(BF16),
                     moe_w2[j].astype(BF16), tm)
            x2 = _combine(x2, y, gts, mod_ffn[l], fg, seq, tm, final)
        else:
            x2, h2 = merged
            x2 = _ffn(x2, h2, mod_ffn[l], ffn_w1[j].astype(BF16), ffn_w3[j].astype(BF16),
                      ffn_w2[j].astype(BF16), fg, seq, tm, FF_TILE, final)
    return x2.reshape(batch, seq, d)
```
